```python
import math
import jax, jax.numpy as jnp
from jax import lax
import numpy as np

D_MODEL = 2048
BATCH = 32
SEQ = 256
DEPTH = 4
DEC_BATCH = 2
DEC_SEQ = 1024
PAST_LEN = 256

GRID_W = 64
N_MIXERS = 4
N_GMLP = (DEPTH + 3) // 4
N_MLA = (DEPTH + 2) // 4
N_DIFF = (DEPTH + 1) // 4
N_POOL = DEPTH // 4
EPS = 1e-6
ROPE_THETA = 10000.0
Q_BLOCK = 128
GMLP_CHUNK = 128
GMLP_WIDTH = D_MODEL
GMLP_GROUPS = 16
GMLP_GROUP_DIM = GMLP_WIDTH // GMLP_GROUPS
MLA_HEADS = D_MODEL // 128
MLA_Q_RANK = D_MODEL // 4
MLA_KV_RANK = D_MODEL // 4
MLA_NOPE = 128
MLA_ROPE = 64
MLA_QK_DIM = MLA_NOPE + MLA_ROPE
MLA_V = 128
DIFF_HEAD_DIM = 128
DIFF_HEADS = D_MODEL // (2 * DIFF_HEAD_DIM)
POOL_WINDOWS = (2, 4, 8, 16)
POOL_GROUPS = 4
POOL_GROUP_DIM = D_MODEL // POOL_GROUPS
N_EXPERTS = 32
TOP_K = 4
EXPERT_DIM = D_MODEL
SWIGLU_LIMIT = 7.0
SWIGLU_ALPHA = 1.702
MOE_BLOCK = 128

kernel_name = "hybrid_flow_trunk_step"


def rms_norm(x, g):
    xf = x.astype(jnp.float32)
    y = xf * lax.rsqrt(jnp.mean(xf * xf, axis=-1, keepdims=True) + EPS)
    return (y * g.astype(jnp.float32)).astype(x.dtype)


def layer_norm(x, g, b):
    xf = x.astype(jnp.float32)
    mu = jnp.mean(xf, axis=-1, keepdims=True)
    var = jnp.mean(jnp.square(xf - mu), axis=-1, keepdims=True)
    y = (xf - mu) * lax.rsqrt(var + EPS)
    return (y * g.astype(jnp.float32) + b.astype(jnp.float32)).astype(x.dtype)


def axial_rope(x):
    L, d = x.shape[1], x.shape[-1]
    rows = L // GRID_W
    row = jnp.repeat(jnp.arange(rows, dtype=jnp.float32), GRID_W)
    col = jnp.tile(jnp.arange(GRID_W, dtype=jnp.float32), rows)
    nf = d // 4
    inv = ROPE_THETA ** (-jnp.arange(nf, dtype=jnp.float32) / nf)
    shape = (1, L) + (1,) * (x.ndim - 3) + (nf,)

    def rot(xa, pos):
        ang = pos[:, None] * inv[None, :]
        cos = jnp.cos(ang).reshape(shape).astype(x.dtype)
        sin = jnp.sin(ang).reshape(shape).astype(x.dtype)
        x1, x2 = jnp.split(xa, 2, axis=-1)
        return jnp.concatenate([x1 * cos - x2 * sin, x1 * sin + x2 * cos], axis=-1)

    xr, xc = jnp.split(x, 2, axis=-1)
    return jnp.concatenate([rot(xr, row), rot(xc, col)], axis=-1)


def rope_tail(x, n_rot):
    return jnp.concatenate([x[..., :-n_rot], axial_rope(x[..., -n_rot:])], axis=-1)


def multi_map_attention(q, k, v, coef, scale):
    B, Lq, H = q.shape[0], q.shape[1], q.shape[2]
    nb = Lq // Q_BLOCK
    qb = jnp.moveaxis(q.reshape((B, nb, Q_BLOCK) + q.shape[2:]), 1, 0)
    coef32 = coef.astype(jnp.float32)

    def one_block(qblk):
        s = jnp.einsum('bqhmd,bkhmd->bhmqk', qblk, k).astype(jnp.float32) * scale
        p = jax.nn.softmax(s, axis=-1)
        p = jnp.einsum('bhmqk,m->bhqk', p, coef32)
        return jnp.einsum('bhqk,bkhe->bqhe', p.astype(v.dtype), v)

    o = lax.map(one_block, qb)
    return jnp.moveaxis(o, 0, 1).reshape(B, Lq, H, v.shape[-1])


def modulation(cond, w_mod, b_mod):
    m = jax.nn.silu(cond) @ w_mod + b_mod
    m = m.reshape(m.shape[:-1] + (6, D_MODEL))
    return tuple(m[..., i, :][..., None, :] for i in range(6))


def pre_norm(x, g, shift, scale):
    return rms_norm(x, g) * (1.0 + scale) + shift


def gmlp_mixer(h, w_in, b_in, ln_g, ln_b, w_s, b_s, w_out, b_out):
    B, L, _ = h.shape
    z = jax.nn.gelu(h @ w_in + b_in, approximate=False)
    u, v = jnp.split(z, 2, axis=-1)
    v = layer_norm(v, ln_g, ln_b)
    v = v.reshape(B, L // GMLP_CHUNK, GMLP_CHUNK, GMLP_GROUPS, GMLP_GROUP_DIM)
    vm = jnp.einsum('gpq,bnqgc->bnpgc', w_s, v) + b_s.T[:, :, None]
    gated = u * vm.reshape(B, L, GMLP_WIDTH)
    return gated @ w_out + b_out


def mla_project(h, w_dq, g_qa, w_uq, w_dkv, g_kva, w_kr, g_q):
    B, L, _ = h.shape
    q = (rms_norm(h @ w_dq, g_qa) @ w_uq).reshape(B, L, MLA_HEADS, MLA_QK_DIM)
    q = rms_norm(q, g_q)
    c_kv = rms_norm(h @ w_dkv, g_kva)
    k_rope = h @ w_kr
    return q, c_kv, k_rope


def mla_expand(c_kv, k_rope, w_uk, w_uv, g_k):
    B, L, _ = c_kv.shape
    k_nope = (c_kv @ w_uk).reshape(B, L, MLA_HEADS, MLA_NOPE)
    k_r = jnp.broadcast_to(k_rope[:, :, None, :], (B, L, MLA_HEADS, MLA_ROPE))
    k = rms_norm(jnp.concatenate([k_nope, k_r], axis=-1), g_k)
    v = (c_kv @ w_uv).reshape(B, L, MLA_HEADS, MLA_V)
    return k, v


def mla_context(h, w_dq, g_qa, w_uq, w_dkv, g_kva, w_kr, w_uk, w_uv, g_q, g_k, w_o):
    B, L, _ = h.shape
    q, c_kv, k_rope = mla_project(h, w_dq, g_qa, w_uq, w_dkv, g_kva, w_kr, g_q)
    k, v = mla_expand(c_kv, k_rope, w_uk, w_uv, g_k)
    o = multi_map_attention(q[:, :, :, None], k[:, :, :, None], v,
                            jnp.ones((1,), jnp.float32), MLA_QK_DIM ** -0.5)
    return o.reshape(B, L, MLA_HEADS * MLA_V) @ w_o, c_kv, k_rope


def mla_latent(h, ckv_ctx, krope_ctx, w_dq, g_qa, w_uq, w_dkv, g_kva, w_kr, w_uk, w_uv, g_q, g_k, w_o):
    B, L, _ = h.shape
    q, c_kv, k_rope = mla_project(h, w_dq, g_qa, w_uq, w_dkv, g_kva, w_kr, g_q)
    k, v = mla_expand(c_kv, k_rope, w_uk, w_uv, g_k)
    q = rope_tail(q, MLA_ROPE)
    k = rope_tail(k, MLA_ROPE)
    kc, vc = mla_expand(ckv_ctx, krope_ctx, w_uk, w_uv, g_k)
    keys = jnp.concatenate([kc, k], axis=1)
    vals = jnp.concatenate([vc, v], axis=1)
    o = multi_map_attention(q[:, :, :, None], keys[:, :, :, None], vals,
                            jnp.ones((1,), jnp.float32), MLA_QK_DIM ** -0.5)
    return o.reshape(B, L, MLA_HEADS * MLA_V) @ w_o


def diff_project(h, w_qkv, g_q, g_k):
    B, L, _ = h.shape
    q, k, v = jnp.split(h @ w_qkv, 3, axis=-1)
    q = rms_norm(q.reshape(B, L, DIFF_HEADS, 2, DIFF_HEAD_DIM), g_q)
    k = rms_norm(k.reshape(B, L, DIFF_HEADS, 2, DIFF_HEAD_DIM), g_k)
    v = v.reshape(B, L, DIFF_HEADS, 2 * DIFF_HEAD_DIM)
    return q, k, v


def diff_coef(lam, lam_init):
    lf = lam.astype(jnp.float32)
    lam_full = jnp.exp(jnp.sum(lf[0] * lf[1])) - jnp.exp(jnp.sum(lf[2] * lf[3])) + lam_init
    return jnp.stack([jnp.ones((), jnp.float32), -lam_full])


def diff_merge(o, g_sub, lam_init, w_o):
    B, L = o.shape[0], o.shape[1]
    o = rms_norm(o, g_sub) * (1.0 - lam_init)
    return o.reshape(B, L, DIFF_HEADS * 2 * DIFF_HEAD_DIM) @ w_o


def diff_context(h, w_qkv, g_q, g_k, lam, g_sub, w_o, lam_init):
    B, L, _ = h.shape
    q, k, v = diff_project(h, w_qkv, g_q, g_k)
    o = multi_map_attention(q, k, v, diff_coef(lam, lam_init), DIFF_HEAD_DIM ** -0.5)
    return diff_merge(o, g_sub, lam_init, w_o), k.reshape(B, L, DIFF_HEADS, 2 * DIFF_HEAD_DIM), v


def diff_latent(h, k_ctx, v_ctx, w_qkv, g_q, g_k, lam, g_sub, w_o, lam_init):
    B, Lc = k_ctx.shape[0], k_ctx.shape[1]
    q, k, v = diff_project(h, w_qkv, g_q, g_k)
    q = axial_rope(q)
    k = axial_rope(k)
    keys = jnp.concatenate([k_ctx.reshape(B, Lc, DIFF_HEADS, 2, DIFF_HEAD_DIM), k], axis=1)
    vals = jnp.concatenate([v_ctx, v], axis=1)
    o = multi_map_attention(q, keys, vals, diff_coef(lam, lam_init), DIFF_HEAD_DIM ** -0.5)
    return diff_merge(o, g_sub, lam_init, w_o)


def pool_mixer(h, w, b, scale):
    B, L, D = h.shape
    hf = h.astype(jnp.float32)
    cs = jnp.concatenate([jnp.zeros((B, 1, D), jnp.float32), jnp.cumsum(hf, axis=1)], axis=1)
    t = jnp.arange(L)
    outs = []
    for gi, win in enumerate(POOL_WINDOWS):
        lo = jnp.clip(t - win // 2, 0, L)
        hi = jnp.clip(t + win // 2, 0, L)
        sl = slice(gi * POOL_GROUP_DIM, (gi + 1) * POOL_GROUP_DIM)
        csg = cs[:, :, sl]
        mean = (csg[:, hi] - csg[:, lo]) / (hi - lo).astype(jnp.float32)[None, :, None]
        outs.append(mean - hf[:, :, sl])
    d = jnp.stack(outs, axis=2).astype(h.dtype)
    y = jnp.einsum('blgc,gcd->blgd', d, w) + b
    return y.reshape(B, L, D) * scale


def moe_ffn(h, w_router, b_router, w_gate_up, b_gate_up, w_down, b_down):
    B, L, D = h.shape
    n = B * L
    xt = h.reshape(n, D)
    logits = (xt @ w_router + b_router).astype(jnp.float32)
    top_val, top_idx = lax.top_k(logits, TOP_K)
    gates = jax.nn.softmax(top_val, axis=-1)
    n_pairs = n * TOP_K
    pair_exp = top_idx.reshape(n_pairs).astype(jnp.int32)
    pair_tok = jnp.repeat(jnp.arange(n, dtype=jnp.int32), TOP_K)
    pair_gate = gates.reshape(n_pairs)
    order = jnp.argsort(pair_exp)
    exp_s, tok_s, gate_s = pair_exp[order], pair_tok[order], pair_gate[order]
    counts = jnp.zeros((N_EXPERTS,), jnp.int32).at[pair_exp].add(1)
    starts = jnp.cumsum(counts) - counts
    padded = (counts + MOE_BLOCK - 1) // MOE_BLOCK * MOE_BLOCK
    pad_end = jnp.cumsum(padded)
    pad_start = pad_end - padded
    dest = pad_start[exp_s] + jnp.arange(n_pairs, dtype=jnp.int32) - starts[exp_s]
    n_rows = -(-(n_pairs + N_EXPERTS * (MOE_BLOCK - 1)) // MOE_BLOCK) * MOE_BLOCK
    n_blocks = n_rows // MOE_BLOCK
    row_tok = jnp.full((n_rows,), n, jnp.int32).at[dest].set(tok_s)
    row_gate = jnp.zeros((n_rows,), jnp.float32).at[dest].set(gate_s)
    blk_exp = jnp.minimum(
        jnp.searchsorted(pad_end, jnp.arange(n_blocks, dtype=jnp.int32) * MOE_BLOCK, side='right'),
        N_EXPERTS - 1).astype(jnp.int32)
    x_pad = jnp.concatenate([xt, jnp.zeros((1, D), xt.dtype)], axis=0)
    xs = x_pad[row_tok].reshape(n_blocks, MOE_BLOCK, D)

    def expert_block(args):
        xb, ei = args
        gu = xb @ w_gate_up[ei] + b_gate_up[ei]
        gate, up = gu[:, :EXPERT_DIM], gu[:, EXPERT_DIM:]
        gate = jnp.minimum(gate, SWIGLU_LIMIT)
        up = jnp.clip(up, -SWIGLU_LIMIT, SWIGLU_LIMIT)
        glu = gate * jax.nn.sigmoid(SWIGLU_ALPHA * gate)
        return ((up + 1.0) * glu) @ w_down[ei] + b_down[ei]

    ys = lax.map(expert_block, (xs, blk_exp)).reshape(n_rows, D)
    y = jnp.zeros((n + 1, D), ys.dtype).at[row_tok].add(ys * row_gate[:, None].astype(ys.dtype))
    return y[:n].reshape(B, L, D)


def setup_inputs(seed: int = 0) -> dict:
    key = jax.random.key(seed)
    ks = iter(jax.random.split(key, 64))

    def nrm(shape, scale):
        return jax.random.normal(next(ks), shape, jnp.float32) * scale

    def gain(shape):
        return 1.0 + nrm(shape, 0.1)

    D = D_MODEL
    return {
        "x_prompt": nrm((BATCH, SEQ, D), 1.0),
        "x_sample": nrm((DEC_BATCH, DEC_SEQ, D), 1.0),
        "c": nrm((DEC_BATCH, D), 1.0),
        "c_ctx": nrm((D,), 1.0),
        "cache_mla_ckv": nrm((DEC_BATCH, N_MLA, PAST_LEN, MLA_KV_RANK), 1.0),
        "cache_mla_krope": nrm((DEC_BATCH, N_MLA, PAST_LEN, MLA_ROPE), 1.0),
        "cache_diff_k": nrm((DEC_BATCH, N_DIFF, PAST_LEN, DIFF_HEADS, 2 * DIFF_HEAD_DIM), 1.0),
        "cache_diff_v": nrm((DEC_BATCH, N_DIFF, PAST_LEN, DIFF_HEADS, 2 * DIFF_HEAD_DIM), 1.0),
        "norm1_g": gain((DEPTH, D)),
        "norm2_g": gain((DEPTH, D)),
        "w_mod": nrm((DEPTH, D, 6 * D), 0.5 * D ** -0.5),
        "b_mod": nrm((DEPTH, 6 * D), 0.02),
        "moe_w_router": nrm((DEPTH, D, N_EXPERTS), D ** -0.5),
        "moe_b_router": nrm((DEPTH, N_EXPERTS), 0.01),
        "moe_w_gate_up": nrm((DEPTH, N_EXPERTS, D, 2 * EXPERT_DIM), D ** -0.5),
        "moe_b_gate_up": nrm((DEPTH, N_EXPERTS, 2 * EXPERT_DIM), 0.02),
        "moe_w_down": nrm((DEPTH, N_EXPERTS, EXPERT_DIM, D), EXPERT_DIM ** -0.5),
        "moe_b_down": nrm((DEPTH, N_EXPERTS, D), 0.02),
        "gmlp_w_in": nrm((N_GMLP, D, 2 * GMLP_WIDTH), D ** -0.5),
        "gmlp_b_in": nrm((N_GMLP, 2 * GMLP_WIDTH), 0.02),
        "gmlp_ln_g": gain((N_GMLP, GMLP_WIDTH)),
        "gmlp_ln_b": nrm((N_GMLP, GMLP_WIDTH), 0.02),
        "gmlp_w_s": nrm((N_GMLP, GMLP_GROUPS, GMLP_CHUNK, GMLP_CHUNK), 0.5 * GMLP_CHUNK ** -0.5),
        "gmlp_b_s": gain((N_GMLP, GMLP_GROUPS, GMLP_CHUNK)),
        "gmlp_w_out": nrm((N_GMLP, GMLP_WIDTH, D), GMLP_WIDTH ** -0.5),
        "gmlp_b_out": nrm((N_GMLP, D), 0.02),
        "mla_w_dq": nrm((N_MLA, D, MLA_Q_RANK), D ** -0.5),
        "mla_g_qa": gain((N_MLA, MLA_Q_RANK)),
        "mla_w_uq": nrm((N_MLA, MLA_Q_RANK, MLA_HEADS * MLA_QK_DIM), MLA_Q_RANK ** -0.5),
        "mla_w_dkv": nrm((N_MLA, D, MLA_KV_RANK), D ** -0.5),
        "mla_g_kva": gain((N_MLA, MLA_KV_RANK)),
        "mla_w_kr": nrm((N_MLA, D, MLA_ROPE), D ** -0.5),
        "mla_w_uk": nrm((N_MLA, MLA_KV_RANK, MLA_HEADS * MLA_NOPE), MLA_KV_RANK ** -0.5),
        "mla_w_uv": nrm((N_MLA, MLA_KV_RANK, MLA_HEADS * MLA_V), MLA_KV_RANK ** -0.5),
        "mla_g_q": gain((N_MLA, MLA_QK_DIM)),
        "mla_g_k": gain((N_MLA, MLA_QK_DIM)),
        "mla_w_o": nrm((N_MLA, MLA_HEADS * MLA_V, D), (MLA_HEADS * MLA_V) ** -0.5),
        "diff_w_qkv": nrm((N_DIFF, D, 3 * DIFF_HEADS * 2 * DIFF_HEAD_DIM), D ** -0.5),
        "diff_g_q": gain((N_DIFF, DIFF_HEAD_DIM)),
        "diff_g_k": gain((N_DIFF, DIFF_HEAD_DIM)),
        "diff_lambda": nrm((N_DIFF, 4, DIFF_HEAD_DIM), 0.1),
        "diff_g_sub": gain((N_DIFF, 2 * DIFF_HEAD_DIM)),
        "diff_w_o": nrm((N_DIFF, DIFF_HEADS * 2 * DIFF_HEAD_DIM, D), (DIFF_HEADS * 2 * DIFF_HEAD_DIM) ** -0.5),
        "pool_w": nrm((N_POOL, POOL_GROUPS, POOL_GROUP_DIM, POOL_GROUP_DIM), POOL_GROUP_DIM ** -0.5),
        "pool_b": nrm((N_POOL, POOL_GROUPS, POOL_GROUP_DIM), 0.02),
        "pool_scale": 0.5 + nrm((N_POOL, D), 0.05),
    }


def reference(x_prompt, x_sample, c, c_ctx, cache_mla_ckv, cache_mla_krope, cache_diff_k, cache_diff_v,
              norm1_g, norm2_g, w_mod, b_mod,
              moe_w_router, moe_b_router, moe_w_gate_up, moe_b_gate_up, moe_w_down, moe_b_down,
              gmlp_w_in, gmlp_b_in, gmlp_ln_g, gmlp_ln_b, gmlp_w_s, gmlp_b_s, gmlp_w_out, gmlp_b_out,
              mla_w_dq, mla_g_qa, mla_w_uq, mla_w_dkv, mla_g_kva, mla_w_kr, mla_w_uk, mla_w_uv,
              mla_g_q, mla_g_k, mla_w_o,
              diff_w_qkv, diff_g_q, diff_g_k, diff_lambda, diff_g_sub, diff_w_o,
              pool_w, pool_b, pool_scale):
    xc = x_prompt
    xl = x_sample
    new_ckv, new_krope, new_dk, new_dv = [], [], [], []
    for l in range(DEPTH):
        kind, j = l % N_MIXERS, l // N_MIXERS
        mc_ = modulation(c_ctx, w_mod[l], b_mod[l])
        ml_ = modulation(c, w_mod[l], b_mod[l])
        hc = pre_norm(xc, norm1_g[l], mc_[0], mc_[1])
        hl = pre_norm(xl, norm1_g[l], ml_[0], ml_[1])
        if kind == 0:
            pa = (gmlp_w_in[j], gmlp_b_in[j], gmlp_ln_g[j], gmlp_ln_b[j], gmlp_w_s[j], gmlp_b_s[j],
                  gmlp_w_out[j], gmlp_b_out[j])
            out_c = gmlp_mixer(hc, *pa)
            out_l = gmlp_mixer(hl, *pa)
        elif kind == 1:
            pb = (mla_w_dq[j], mla_g_qa[j], mla_w_uq[j], mla_w_dkv[j], mla_g_kva[j], mla_w_kr[j],
                  mla_w_uk[j], mla_w_uv[j], mla_g_q[j], mla_g_k[j], mla_w_o[j])
            out_c, ckv, krope = mla_context(hc, *pb)
            out_l = mla_latent(hl, cache_mla_ckv[:, j], cache_mla_krope[:, j], *pb)
            new_ckv.append(ckv)
            new_krope.append(krope)
        elif kind == 2:
            lam_init = 0.8 - 0.6 * math.exp(-0.3 * l)
            pd = (diff_w_qkv[j], diff_g_q[j], diff_g_k[j], diff_lambda[j], diff_g_sub[j], diff_w_o[j], lam_init)
            out_c, dk, dv = diff_context(hc, *pd)
            out_l = diff_latent(hl, cache_diff_k[:, j], cache_diff_v[:, j], *pd)
            new_dk.append(dk)
            new_dv.append(dv)
        else:
            pp = (pool_w[j], pool_b[j], pool_scale[j])
            out_c = pool_mixer(hc, *pp)
            out_l = pool_mixer(hl, *pp)
        xc = xc + mc_[2] * out_c
        xl = xl + ml_[2] * out_l
        pm = (moe_w_router[l], moe_b_router[l], moe_w_gate_up[l], moe_b_gate_up[l], moe_w_down[l], moe_b_down[l])
        hc = pre_norm(xc, norm2_g[l], mc_[3], mc_[4])
        hl = pre_norm(xl, norm2_g[l], ml_[3], ml_[4])
        xc = xc + mc_[5] * moe_ffn(hc, *pm)
        xl = xl + ml_[5] * moe_ffn(hl, *pm)
    state_mla_ckv = jnp.stack(new_ckv, axis=1)
    state_mla_krope = jnp.stack(new_krope, axis=1)
    state_diff_k = jnp.stack(new_dk, axis=1)
    state_diff_v = jnp.stack(new_dv, axis=1)
    return (xc, xl, state_mla_ckv, state_mla_krope, state_diff_k, state_diff_v)
```

```python
import collections
import functools
import math

import jax
import jax.numpy as jnp
from jax import lax
from jax.experimental import pallas as pl
from jax.experimental.pallas import tpu as pltpu

F32 = jnp.float32
BF16 = jnp.bfloat16
I32 = jnp.int32

D = 2048
EPS = 1e-6
ROPE_THETA = 10000.0
GRID_W = 64
TM = 256
LANES = 128
VMEM_LIMIT = 56 * 1024 * 1024

GMLP_CHUNK = 128
GMLP_GROUPS = 16
MLA_HEADS = 16
MLA_RANK = 512
MLA_NOPE = 128
MLA_ROPE = 64
MLA_QK = MLA_NOPE + MLA_ROPE
DIFF_HEADS = 8
DIFF_DIM = 128
POOL_WINDOWS = (2, 4, 8, 16)
POOL_GROUP_DIM = 512
N_EXPERTS = 32
TOP_K = 4
EXPERT_DIM = 2048
SWIGLU_LIMIT = 7.0
SWIGLU_ALPHA = 1.702
MOE_BM = 256
MOE_BN = 512

Geom = collections.namedtuple("Geom", "n_ctx ctx_len n_lat lat_len past_len")


def _ctx_rows(g):
    return g.n_ctx * g.ctx_len


def _lat_rows(g):
    return g.n_lat * g.lat_len


def _ctx_tiles(g):
    return _ctx_rows(g) // TM


def _mod_row(g, i):
    ct = _ctx_tiles(g)
    return jnp.where(i < ct, 0, 1 + (i - ct) // (g.lat_len // TM))


def _rope_tile(g, i):
    ct = _ctx_tiles(g)
    return jnp.where(i < ct, 0, 1 + (i - ct) % (g.lat_len // TM))


def _params(sem):
    return pltpu.CompilerParams(dimension_semantics=sem, vmem_limit_bytes=VMEM_LIMIT)


def _mod_spec(g, k, n_axis=None, bn=D):
    per = D // bn
    if n_axis is None:
        return pl.BlockSpec((None, 1, bn), lambda m: (_mod_row(g, m), 0, k * per))
    return pl.BlockSpec((None, 1, bn), lambda n, m: (_mod_row(g, m), 0, k * per + n))


def _rms(x, gain):
    return x * lax.rsqrt(jnp.mean(x * x, axis=-1, keepdims=True) + EPS) * gain


def _prenorm(x, gain, shift, scale):
    return _rms(x, gain) * (1.0 + scale) + shift


def _modulation_kernel(c_ref, w_ref, b_ref, o_ref):
    c = c_ref[...]
    s = c * (1.0 / (1.0 + jnp.exp(-c)))
    o_ref[...] = jnp.dot(s.astype(BF16), w_ref[...].astype(BF16), preferred_element_type=F32) + b_ref[...]


def _modulation(cond8, w_mod, b_mod):
    depth = w_mod.shape[0]
    bn = 1024
    out = pl.pallas_call(
        _modulation_kernel,
        grid=(depth, 6 * D // bn),
        in_specs=[
            pl.BlockSpec((8, D), lambda l, n: (0, 0)),
            pl.BlockSpec((None, D, bn), lambda l, n: (l, 0, n)),
            pl.BlockSpec((None, 1, bn), lambda l, n: (l, 0, n)),
        ],
        out_specs=pl.BlockSpec((None, 8, bn), lambda l, n: (l, 0, n)),
        out_shape=jax.ShapeDtypeStruct((depth, 8, 6 * D), F32),
        compiler_params=_params(("arbitrary", "arbitrary")),
        name="modulation",
    )(cond8, w_mod, b_mod.reshape(depth, 1, 6 * D))
    return out.reshape(depth, 8, 1, 6 * D)


def _prenorm_kernel(x_ref, g_ref, sh_ref, sc_ref, o_ref):
    o_ref[...] = _prenorm(x_ref[...], g_ref[...], sh_ref[...], sc_ref[...]).astype(o_ref.dtype)


def _prenorm_call(g, x, gain, mod, k_shift):
    m = x.shape[0]
    return pl.pallas_call(
        _prenorm_kernel,
        grid=(m // TM,),
        in_specs=[
            pl.BlockSpec((TM, D), lambda i: (i, 0)),
            pl.BlockSpec((1, D), lambda i: (0, 0)),
            _mod_spec(g, k_shift),
            _mod_spec(g, k_shift + 1),
        ],
        out_specs=pl.BlockSpec((TM, D), lambda i: (i, 0)),
        out_shape=jax.ShapeDtypeStruct((m, D), BF16),
        compiler_params=_params(("arbitrary",)),
        name="prenorm",
    )(x, gain.reshape(1, D), mod, mod)


def _mm_kernel(n_extra, epilogue, x_ref, w_ref, *rest):
    extras, outs, wbf_ref = rest[:n_extra], rest[n_extra:-1], rest[-1]

    @pl.when(pl.program_id(1) == 0)
    def _():
        wbf_ref[...] = w_ref[...].astype(BF16)

    acc = jnp.dot(x_ref[...].astype(BF16), wbf_ref[...], preferred_element_type=F32)
    epilogue(acc, extras, outs)


def _mm(x, w, ncols, bn, epilogue, extras, out_shape, out_specs, name, w_off=0):
    m, k = x.shape
    return pl.pallas_call(
        functools.partial(_mm_kernel, len(extras), epilogue),
        grid=(ncols // bn, m // TM),
        in_specs=[
            pl.BlockSpec((TM, k), lambda n, i: (i, 0)),
            pl.BlockSpec((k, bn), lambda n, i: (0, n + w_off)),
        ]
        + [s for _, s in extras],
        out_specs=out_specs,
        out_shape=out_shape,
        scratch_shapes=[pltpu.VMEM((k, bn), BF16)],
        compiler_params=_params(("arbitrary", "arbitrary")),
        name=name,
    )(x, w, *[a for a, _ in extras])


def _row_spec(bn):
    return pl.BlockSpec((1, bn), lambda n, i: (0, n))


def _tile_spec(bn):
    return pl.BlockSpec((TM, bn), lambda n, i: (i, n))


def _residual_epilogue(acc, extras, outs):
    b_ref, x_ref, gate_ref = extras
    outs[0][...] = x_ref[...] + gate_ref[...] * (acc + b_ref[...])


def _mm_residual(g, h, w, bias, x, mod, k_gate, name):
    m = x.shape[0]
    bn = 512
    extras = [(bias.reshape(1, D), _row_spec(bn)), (x, _tile_spec(bn)), (mod, _mod_spec(g, k_gate, 0, bn))]
    return _mm(h, w, D, bn, _residual_epilogue, extras, jax.ShapeDtypeStruct((m, D), F32), _tile_spec(bn), name)


def _gelu_epilogue(acc, extras, outs):
    z = acc + extras[0][...]
    outs[0][...] = (0.5 * z * (1.0 + lax.erf(z * (2.0 ** -0.5)))).astype(BF16)


def _gmlp_gate_kernel(u_ref, v_ref, lg_ref, lb_ref, ws_ref, bs_ref, o_ref):
    v = v_ref[...].astype(F32)
    mu = jnp.mean(v, axis=-1, keepdims=True)
    vc = v - mu
    var = jnp.mean(vc * vc, axis=-1, keepdims=True)
    vn = (vc * lax.rsqrt(var + EPS) * lg_ref[...] + lb_ref[...]).astype(BF16)
    for grp in range(GMLP_GROUPS):
        cols = slice(grp * LANES, (grp + 1) * LANES)
        w = ws_ref[grp].astype(BF16)
        bias = bs_ref[:, grp : grp + 1]
        for c in range(TM // GMLP_CHUNK):
            rows = slice(c * GMLP_CHUNK, (c + 1) * GMLP_CHUNK)
            vm = jnp.dot(w, vn[rows, cols], preferred_element_type=F32) + bias
            o_ref[rows, cols] = (u_ref[rows, cols].astype(F32) * vm).astype(BF16)


def _gmlp_layer(g, x, mod, norm_g, w_in, b_in, ln_g, ln_b, w_s, b_s, w_out, b_out):
    m = x.shape[0]
    width = D
    h = _prenorm_call(g, x, norm_g, mod, 0)
    bn = 512
    z = _mm(h, w_in, 2 * width, bn, _gelu_epilogue, [(b_in.reshape(1, 2 * width), _row_spec(bn))],
            jax.ShapeDtypeStruct((m, 2 * width), BF16), _tile_spec(bn), "gmlp_in")
    gated = pl.pallas_call(
        _gmlp_gate_kernel,
        grid=(m // TM,),
        in_specs=[
            pl.BlockSpec((TM, width), lambda i: (i, 0)),
            pl.BlockSpec((TM, width), lambda i: (i, 1)),
            pl.BlockSpec((1, width), lambda i: (0, 0)),
            pl.BlockSpec((1, width), lambda i: (0, 0)),
            pl.BlockSpec((GMLP_GROUPS, GMLP_CHUNK, GMLP_CHUNK), lambda i: (0, 0, 0)),
            pl.BlockSpec((GMLP_CHUNK, GMLP_GROUPS), lambda i: (0, 0)),
        ],
        out_specs=pl.BlockSpec((TM, width), lambda i: (i, 0)),
        out_shape=jax.ShapeDtypeStruct((m, width), BF16),
        compiler_params=_params(("arbitrary",)),
        name="gmlp_gate",
    )(z, z, ln_g.reshape(1, width), ln_b.reshape(1, width), w_s, b_s.T)
    return _mm_residual(g, gated, w_out, b_out, x, mod, 2, "gmlp_out")


def _rope_tables(g, d):
    nf = d // 4
    t = jnp.arange(g.lat_len)
    row = (t // GRID_W).astype(F32)
    col = (t % GRID_W).astype(F32)
    inv = ROPE_THETA ** (-jnp.arange(nf, dtype=F32) / nf)
    ang_r = row[:, None] * inv[None, :]
    ang_c = col[:, None] * inv[None, :]
    cos = jnp.concatenate([jnp.cos(ang_r)] * 2 + [jnp.cos(ang_c)] * 2, axis=-1)
    sin = jnp.concatenate([-jnp.sin(ang_r), jnp.sin(ang_r), -jnp.sin(ang_c), jnp.sin(ang_c)], axis=-1)
    reps = LANES // d
    cos = jnp.tile(cos, (1, reps))
    sin = jnp.tile(sin, (1, reps))
    cos = jnp.concatenate([jnp.ones((TM, LANES), F32), cos], axis=0)
    sin = jnp.concatenate([jnp.zeros((TM, LANES), F32), sin], axis=0)
    return cos, sin


def _rope_specs(g):
    return pl.BlockSpec((TM, LANES), lambda n, i: (_rope_tile(g, i), 0))


def _rope128(x, cos, sin, nf):
    lane = lax.broadcasted_iota(I32, x.shape, 1)
    swapped = jnp.where((lane % (2 * nf)) < nf, pltpu.roll(x, LANES - nf, 1), pltpu.roll(x, nf, 1))
    return x * cos + swapped * sin


def _softmax_rows(s):
    s = s - jnp.max(s, axis=-1, keepdims=True)
    p = jnp.exp(s)
    return p / jnp.sum(p, axis=-1, keepdims=True)


def _dot_t(a, b):
    return lax.dot_general(a, b, (((1,), (1,)), ((), ())), preferred_element_type=F32)


def _rmsnorm_epilogue(acc, extras, outs):
    outs[0][...] = _rms(acc, extras[0][...]).astype(outs[0].dtype)


def _plain_epilogue(acc, extras, outs):
    outs[0][...] = acc.astype(outs[0].dtype)


def _pair_select(lane_lo, a, b):
    return jnp.where(lane_lo, a, b)


def _mla_q_epilogue(acc, extras, outs):
    gn_ref, gr_ref, cos_ref, sin_ref = extras
    qn_ref, qr_ref = outs
    nope_w = MLA_HEADS * MLA_NOPE
    lane_lo = lax.broadcasted_iota(I32, (TM, LANES), 1) < MLA_ROPE
    cos, sin = cos_ref[...], sin_ref[...]
    for pair in range(MLA_HEADS // 2):
        r = acc[:, nope_w + pair * LANES : nope_w + (pair + 1) * LANES]
        r2 = r * r
        ss_lo = jnp.sum(jnp.where(lane_lo, r2, 0.0), axis=-1, keepdims=True)
        ss_hi = jnp.sum(jnp.where(lane_lo, 0.0, r2), axis=-1, keepdims=True)
        rinv = []
        for j, ss_r in enumerate((ss_lo, ss_hi)):
            h = 2 * pair + j
            qn = acc[:, h * LANES : (h + 1) * LANES]
            ri = lax.rsqrt((jnp.sum(qn * qn, axis=-1, keepdims=True) + ss_r) * (1.0 / MLA_QK) + EPS)
            qn_ref[:, h * LANES : (h + 1) * LANES] = (qn * ri * gn_ref[...]).astype(BF16)
            rinv.append(ri)
        rn = r * _pair_select(lane_lo, rinv[0], rinv[1]) * gr_ref[...]
        qr_ref[:, pair * LANES : (pair + 1) * LANES] = _rope128(rn, cos, sin, MLA_ROPE // 4).astype(BF16)


def _mla_kv_epilogue(acc, extras, outs):
    kr_in_ref, gn_ref, gr_ref, cos_ref, sin_ref = extras
    kn_ref, kr_ref, v_ref = outs
    nope_w = MLA_HEADS * MLA_NOPE
    lane_lo = lax.broadcasted_iota(I32, (TM, LANES), 1) < MLA_ROPE
    kr = kr_in_ref[...]
    ss_r = jnp.sum(kr * kr, axis=-1, keepdims=True)
    kr2 = jnp.concatenate([kr, kr], axis=-1) * gr_ref[...]
    kr2 = _rope128(kr2, cos_ref[...], sin_ref[...], MLA_ROPE // 4)
    rinv = []
    for h in range(MLA_HEADS):
        kn = acc[:, h * LANES : (h + 1) * LANES]
        ri = lax.rsqrt((jnp.sum(kn * kn, axis=-1, keepdims=True) + ss_r) * (1.0 / MLA_QK) + EPS)
        kn_ref[:, h * LANES : (h + 1) * LANES] = (kn * ri * gn_ref[...]).astype(BF16)
        rinv.append(ri)
    for pair in range(MLA_HEADS // 2):
        scale = _pair_select(lane_lo, rinv[2 * pair], rinv[2 * pair + 1])
        kr_ref[:, pair * LANES : (pair + 1) * LANES] = (kr2 * scale).astype(BF16)
    v_ref[...] = acc[:, nope_w:].astype(BF16)


def _mla_attn_kernel(qn_ref, qr_ref, kn_ref, kr_ref, v_ref, o_ref):
    scale = MLA_QK ** -0.5
    for h in range(MLA_HEADS):
        cn = slice(h * MLA_NOPE, (h + 1) * MLA_NOPE)
        cr = slice(h * MLA_ROPE, (h + 1) * MLA_ROPE)
        s = _dot_t(qn_ref[:, cn], kn_ref[:, cn]) + _dot_t(qr_ref[:, cr], kr_ref[:, cr])
        p = _softmax_rows(s * scale)
        o_ref[:, cn] = jnp.dot(p.astype(BF16), v_ref[:, cn], preferred_element_type=F32).astype(BF16)


def _attention_call(kernel, q_arrays, kv_arrays, n_seq, q_len, kv_len, q_row0, out_width, extras, name):
    qb = q_len // TM
    q_specs = [pl.BlockSpec((TM, a.shape[1]), lambda s, j: (q_row0 // TM + s * qb + j, 0)) for a in q_arrays]
    kv_specs = [pl.BlockSpec((kv_len, a.shape[1]), lambda s, j: (s, 0)) for a in kv_arrays]
    return pl.pallas_call(
        kernel,
        grid=(n_seq, qb),
        in_specs=q_specs + kv_specs + [s for _, s in extras],
        out_specs=pl.BlockSpec((TM, out_width), lambda s, j: (s * qb + j, 0)),
        out_shape=jax.ShapeDtypeStruct((n_seq * q_len, out_width), BF16),
        compiler_params=_params(("arbitrary", "arbitrary")),
        name=name,
    )(*q_arrays, *kv_arrays, *[a for a, _ in extras])


def _latent_kv(g, own, cache):
    nc = _ctx_rows(g)
    own = own[nc:].reshape(g.n_lat, g.lat_len, own.shape[1])
    cache = cache.reshape(g.n_lat, g.past_len, cache.shape[1])
    return jnp.concatenate([cache, own], axis=1).reshape(g.n_lat * (g.past_len + g.lat_len), own.shape[2])


def _mla_layer(g, x, mod, norm_g, cache_ckv, cache_krope, w_dq, g_qa, w_uq, w_dkv, g_kva, w_kr, w_uk, w_uv,
               g_q, g_k, w_o):
    m = x.shape[0]
    nc, nl = _ctx_rows(g), _lat_rows(g)
    h = _prenorm_call(g, x, norm_g, mod, 0)
    cos, sin = _rope_tables(g, MLA_ROPE)
    rope_extras = [(cos, _rope_specs(g)), (sin, _rope_specs(g))]
    full = lambda w: pl.BlockSpec((1, w), lambda n, i: (0, 0))

    qa = _mm(h, w_dq, MLA_RANK, MLA_RANK, _rmsnorm_epilogue, [(g_qa.reshape(1, MLA_RANK), full(MLA_RANK))],
             jax.ShapeDtypeStruct((m, MLA_RANK), BF16), _tile_spec(MLA_RANK), "mla_dq")
    ckv = _mm(h, w_dkv, MLA_RANK, MLA_RANK, _rmsnorm_epilogue, [(g_kva.reshape(1, MLA_RANK), full(MLA_RANK))],
              jax.ShapeDtypeStruct((m, MLA_RANK), F32), _tile_spec(MLA_RANK), "mla_dkv")
    krope = _mm(h, w_kr, MLA_ROPE, MLA_ROPE, _plain_epilogue, [],
                jax.ShapeDtypeStruct((m, MLA_ROPE), F32), _tile_spec(MLA_ROPE), "mla_kr")

    w_uq3 = w_uq.reshape(MLA_RANK, MLA_HEADS, MLA_QK)
    w_uq_p = jnp.concatenate([w_uq3[:, :, :MLA_NOPE].reshape(MLA_RANK, -1),
                              w_uq3[:, :, MLA_NOPE:].reshape(MLA_RANK, -1)], axis=1)
    qw = w_uq_p.shape[1]
    gq_n = g_q[:MLA_NOPE].reshape(1, MLA_NOPE)
    gq_r = jnp.tile(g_q[MLA_NOPE:], 2).reshape(1, LANES)
    nope_w, rope_w = MLA_HEADS * MLA_NOPE, MLA_HEADS * MLA_ROPE
    qn, qr = _mm(qa, w_uq_p, qw, qw, _mla_q_epilogue,
                 [(gq_n, full(LANES)), (gq_r, full(LANES))] + rope_extras,
                 (jax.ShapeDtypeStruct((m, nope_w), BF16), jax.ShapeDtypeStruct((m, rope_w), BF16)),
                 (pl.BlockSpec((TM, nope_w), lambda n, i: (i, 0)), pl.BlockSpec((TM, rope_w), lambda n, i: (i, 0))),
                 "mla_uq")

    n_cache = g.n_lat * g.past_len
    ckv_all = jnp.concatenate([ckv, cache_ckv.reshape(n_cache, MLA_RANK)], axis=0)
    kr_all = jnp.concatenate([krope, cache_krope.reshape(n_cache, MLA_ROPE)], axis=0)
    w_ukv = jnp.concatenate([w_uk, w_uv], axis=1)
    gk_n = g_k[:MLA_NOPE].reshape(1, MLA_NOPE)
    gk_r = jnp.tile(g_k[MLA_NOPE:], 2).reshape(1, LANES)
    m_all = m + n_cache
    n_tok_tiles = m // TM
    kv_rope = pl.BlockSpec((TM, LANES), lambda n, i: (jnp.where(i < n_tok_tiles, _rope_tile(g, i), 0), 0))
    kn, kr, v = _mm(ckv_all, w_ukv, 2 * nope_w, 2 * nope_w, _mla_kv_epilogue,
                    [(kr_all, pl.BlockSpec((TM, MLA_ROPE), lambda n, i: (i, 0))), (gk_n, full(LANES)),
                     (gk_r, full(LANES)), (cos, kv_rope), (sin, kv_rope)],
                    (jax.ShapeDtypeStruct((m_all, nope_w), BF16), jax.ShapeDtypeStruct((m_all, rope_w), BF16),
                     jax.ShapeDtypeStruct((m_all, nope_w), BF16)),
                    (pl.BlockSpec((TM, nope_w), lambda n, i: (i, 0)), pl.BlockSpec((TM, rope_w), lambda n, i: (i, 0)),
                     pl.BlockSpec((TM, nope_w), lambda n, i: (i, 0))),
                    "mla_ukv")

    o_ctx = _attention_call(_mla_attn_kernel, [qn, qr], [kn, kr, v], g.n_ctx, g.ctx_len, g.ctx_len, 0,
                            nope_w, [], "mla_attn_ctx")
    lat_kv = [_latent_kv(g, a[:m], a[m:]) for a in (kn, kr, v)]
    o_lat = _attention_call(_mla_attn_kernel, [qn, qr], lat_kv, g.n_lat, g.lat_len, g.past_len + g.lat_len, nc,
                            nope_w, [], "mla_attn_lat")
    o = jnp.concatenate([o_ctx, o_lat], axis=0)
    x = _mm_residual(g, o, w_o, jnp.zeros((D,), F32), x, mod, 2, "mla_out")
    return x, ckv[:nc], krope[:nc]


def _diff_qk_epilogue(acc, extras, outs):
    g_ref, cos_ref, sin_ref = extras
    cos, sin = cos_ref[...], sin_ref[...]
    for j in range(acc.shape[1] // LANES):
        cols = slice(j * LANES, (j + 1) * LANES)
        y = _rms(acc[:, cols], g_ref[...])
        if len(outs) == 2:
            outs[1][:, cols] = y
        outs[0][:, cols] = _rope128(y, cos, sin, DIFF_DIM // 4).astype(BF16)


def _diff_v_epilogue(acc, extras, outs):
    outs[0][...] = acc.astype(BF16)
    outs[1][...] = acc


def _diff_attn_kernel(lam_init, q_ref, k_ref, v_ref, lam_ref, gs_ref, o_ref):
    lam = lam_ref[...]
    lam_full = (jnp.exp(jnp.sum(lam[0:1] * lam[1:2], axis=-1, keepdims=True))
                - jnp.exp(jnp.sum(lam[2:3] * lam[3:4], axis=-1, keepdims=True)) + lam_init)
    scale = DIFF_DIM ** -0.5
    for h in range(DIFF_HEADS):
        c0 = slice(2 * h * DIFF_DIM, (2 * h + 1) * DIFF_DIM)
        c1 = slice((2 * h + 1) * DIFF_DIM, (2 * h + 2) * DIFF_DIM)
        cv = slice(2 * h * DIFF_DIM, (2 * h + 2) * DIFF_DIM)
        p0 = _softmax_rows(_dot_t(q_ref[:, c0], k_ref[:, c0]) * scale)
        p1 = _softmax_rows(_dot_t(q_ref[:, c1], k_ref[:, c1]) * scale)
        p = p0 - lam_full * p1
        o = jnp.dot(p.astype(BF16), v_ref[:, cv], preferred_element_type=F32)
        o_ref[:, cv] = (_rms(o, gs_ref[...]) * (1.0 - lam_init)).astype(BF16)


def _diff_layer(g, x, mod, norm_g, cache_k, cache_v, w_qkv, g_q, g_k, lam, g_sub, w_o, lam_init):
    m = x.shape[0]
    nc = _ctx_rows(g)
    h = _prenorm_call(g, x, norm_g, mod, 0)
    cos, sin = _rope_tables(g, DIFF_DIM)
    rope_extras = [(cos, _rope_specs(g)), (sin, _rope_specs(g))]
    bn = 512
    gain = lambda a: (a.reshape(1, DIFF_DIM), pl.BlockSpec((1, DIFF_DIM), lambda n, i: (0, 0)))
    (q,) = _mm(h, w_qkv, D, bn, _diff_qk_epilogue, [gain(g_q)] + rope_extras,
               (jax.ShapeDtypeStruct((m, D), BF16),), (_tile_spec(bn),), "diff_q")
    k, k_state = _mm(h, w_qkv, D, bn, _diff_qk_epilogue, [gain(g_k)] + rope_extras,
                     (jax.ShapeDtypeStruct((m, D), BF16), jax.ShapeDtypeStruct((m, D), F32)),
                     (_tile_spec(bn), _tile_spec(bn)), "diff_k", w_off=D // bn)
    v, v_state = _mm(h, w_qkv, D, bn, _diff_v_epilogue, [],
                     (jax.ShapeDtypeStruct((m, D), BF16), jax.ShapeDtypeStruct((m, D), F32)),
                     (_tile_spec(bn), _tile_spec(bn)), "diff_v", w_off=2 * D // bn)
    extras = [(lam, pl.BlockSpec((4, DIFF_DIM), lambda s, j: (0, 0))),
              (g_sub.reshape(1, 2 * DIFF_DIM), pl.BlockSpec((1, 2 * DIFF_DIM), lambda s, j: (0, 0)))]
    kern = functools.partial(_diff_attn_kernel, lam_init)
    o_ctx = _attention_call(kern, [q], [k, v], g.n_ctx, g.ctx_len, g.ctx_len, 0, D, extras, "diff_attn_ctx")
    n_cache = g.n_lat * g.past_len
    k_lat = _latent_kv(g, k, cache_k.reshape(n_cache, D).astype(BF16))
    v_lat = _latent_kv(g, v, cache_v.reshape(n_cache, D).astype(BF16))
    o_lat = _attention_call(kern, [q], [k_lat, v_lat], g.n_lat, g.lat_len, g.past_len + g.lat_len, nc, D,
                            extras, "diff_attn_lat")
    o = jnp.concatenate([o_ctx, o_lat], axis=0)
    x = _mm_residual(g, o, w_o, jnp.zeros((D,), F32), x, mod, 2, "diff_out")
    return x, k_state[:nc], v_state[:nc]


def _pool_kernel(h_ref, x_ref, gate_ref, a_ref, ic_ref, w_ref, b_ref, ps_ref, o_ref):
    hb = h_ref[...]
    win_sum = jnp.dot(a_ref[...], hb, preferred_element_type=F32)
    d = win_sum * ic_ref[...] - hb.astype(F32)
    y = jnp.dot(d.astype(BF16), w_ref[...].astype(BF16), preferred_element_type=F32) + b_ref[...]
    o_ref[...] = x_ref[...] + gate_ref[...] * (y * ps_ref[...])


def _pool_stream(h, x, n_seq, seq_len, row0, mod_row0, mod_rows_per_seq, mod, w, b, scale):
    t = jnp.arange(seq_len)
    bands, inv_counts = [], []
    for win in POOL_WINDOWS:
        lo = jnp.clip(t - win // 2, 0, seq_len)
        hi = jnp.clip(t + win // 2, 0, seq_len)
        bands.append(((t[None, :] >= lo[:, None]) & (t[None, :] < hi[:, None])).astype(BF16))
        inv_counts.append((1.0 / (hi - lo).astype(F32))[:, None])
    band = jnp.stack(bands)
    inv_count = jnp.stack(inv_counts)
    gd = POOL_GROUP_DIM
    seq0 = row0 // seq_len
    per = D // gd
    return pl.pallas_call(
        _pool_kernel,
        grid=(n_seq, len(POOL_WINDOWS)),
        in_specs=[
            pl.BlockSpec((seq_len, gd), lambda s, gi: (seq0 + s, gi)),
            pl.BlockSpec((seq_len, gd), lambda s, gi: (seq0 + s, gi)),
            pl.BlockSpec((None, 1, gd), lambda s, gi: (mod_row0 + s * mod_rows_per_seq, 0, 2 * per + gi)),
            pl.BlockSpec((None, seq_len, seq_len), lambda s, gi: (gi, 0, 0)),
            pl.BlockSpec((None, seq_len, 1), lambda s, gi: (gi, 0, 0)),
            pl.BlockSpec((None, gd, gd), lambda s, gi: (gi, 0, 0)),
            pl.BlockSpec((1, gd), lambda s, gi: (0, gi)),
            pl.BlockSpec((1, gd), lambda s, gi: (0, gi)),
        ],
        out_specs=pl.BlockSpec((seq_len, gd), lambda s, gi: (s, gi)),
        out_shape=jax.ShapeDtypeStruct((n_seq * seq_len, D), F32),
        compiler_params=_params(("arbitrary", "arbitrary")),
        name="pool",
    )(h, x, mod, band, inv_count, w, b.reshape(1, D), scale.reshape(1, D))


def _pool_layer(g, x, mod, norm_g, w, b, scale):
    nc = _ctx_rows(g)
    h = _prenorm_call(g, x, norm_g, mod, 0)
    xc = _pool_stream(h, x, g.n_ctx, g.ctx_len, 0, 0, 0, mod, w, b, scale)
    xl = _pool_stream(h, x, g.n_lat, g.lat_len, nc, 1, 1, mod, w, b, scale)
    return jnp.concatenate([xc, xl], axis=0)


def _split_bf16(a):
    hi = a.astype(BF16)
    return hi, (a - hi.astype(F32)).astype(BF16)


def _router_kernel(x_ref, g_ref, sh_ref, sc_ref, wr_ref, br_ref, h_ref, idx_ref, pos_ref, gate_ref, cnt_ref,
                   carry_ref):
    @pl.when(pl.program_id(0) == 0)
    def _():
        carry_ref[...] = jnp.zeros_like(carry_ref)

    h = _prenorm(x_ref[...], g_ref[...], sh_ref[...], sc_ref[...])
    h_ref[...] = h
    h_hi, h_lo = _split_bf16(h)
    w_hi, w_lo = _split_bf16(wr_ref[...])
    dot = functools.partial(jnp.dot, preferred_element_type=F32)
    logits = dot(h_hi, w_hi) + dot(h_hi, w_lo) + dot(h_lo, w_hi) + br_ref[...]

    lane = lax.broadcasted_iota(I32, logits.shape, 1).astype(F32)
    vals, hots = [], []
    for k in range(TOP_K):
        top = jnp.max(logits, axis=-1, keepdims=True)
        sel = jnp.min(jnp.where(logits == top, lane, float(N_EXPERTS)), axis=-1, keepdims=True)
        hot = lane == sel
        idx_ref[:, k : k + 1] = sel.astype(I32)
        vals.append(top)
        hots.append(hot)
        logits = jnp.where(hot, -jnp.inf, logits)
    exps = [jnp.exp(v - vals[0]) for v in vals]
    denom = exps[0] + exps[1] + exps[2] + exps[3]
    for k in range(TOP_K):
        gate_ref[:, k : k + 1] = exps[k] / denom

    hot_all = sum(jnp.where(hot, 1.0, 0.0) for hot in hots)
    r = lax.broadcasted_iota(I32, (TM, TM), 0)
    c = lax.broadcasted_iota(I32, (TM, TM), 1)
    earlier = jnp.where(r > c, 1.0, 0.0).astype(BF16)
    base = dot(earlier, hot_all.astype(BF16)) + carry_ref[...]
    for k in range(TOP_K):
        pos_ref[:, k : k + 1] = jnp.sum(jnp.where(hots[k], base, 0.0), axis=-1, keepdims=True).astype(I32)
    carry_ref[...] = carry_ref[...] + jnp.sum(hot_all, axis=0, keepdims=True)
    cnt_ref[...] = carry_ref[...]


def _dispatch_kernel(dest_ref, h_ref, xs_ref, sem):
    base = pl.program_id(0) * (TM * TOP_K)

    def copy(r, k):
        d = dest_ref[base + r * TOP_K + k]
        return pltpu.make_async_copy(h_ref.at[pl.ds(r, 1), :], xs_ref.at[pl.ds(d, 1), :], sem)

    def issue(r, carry):
        for k in range(TOP_K):
            copy(r, k).start()
        return carry

    def drain(r, carry):
        for k in range(TOP_K):
            copy(r, k).wait()
        return carry

    lax.fori_loop(0, TM, issue, 0)
    lax.fori_loop(0, TM, drain, 0)


def _expert_up_kernel(be_ref, br_ref, first_ref, nv_ref, na_ref, x_ref, wg_ref, wu_ref, bg_ref, bu_ref, o_ref,
                      wg_bf, wu_bf):
    b = pl.program_id(1)
    active = b < na_ref[0]

    @pl.when(jnp.logical_and(active, first_ref[b] == 1))
    def _():
        wg_bf[...] = wg_ref[...].astype(BF16)
        wu_bf[...] = wu_ref[...].astype(BF16)

    @pl.when(active)
    def _():
        row = lax.broadcasted_iota(I32, (MOE_BM, 1), 0)
        x = jnp.where(row < nv_ref[b], x_ref[...], 0.0).astype(BF16)
        gate = jnp.dot(x, wg_bf[...], preferred_element_type=F32) + bg_ref[...]
        up = jnp.dot(x, wu_bf[...], preferred_element_type=F32) + bu_ref[...]
        gate = jnp.minimum(gate, SWIGLU_LIMIT)
        up = jnp.clip(up, -SWIGLU_LIMIT, SWIGLU_LIMIT)
        glu = gate * (1.0 / (1.0 + jnp.exp(-SWIGLU_ALPHA * gate)))
        o_ref[...] = ((up + 1.0) * glu).astype(BF16)


def _expert_down_kernel(be_ref, br_ref, first_ref, nv_ref, na_ref, h_ref, w_ref, b_ref, o_ref, w_bf):
    b = pl.program_id(1)
    active = b < na_ref[0]

    @pl.when(jnp.logical_and(active, first_ref[b] == 1))
    def _():
        w_bf[...] = w_ref[...].astype(BF16)

    @pl.when(active)
    def _():
        o_ref[...] = jnp.dot(h_ref[...], w_bf[...], preferred_element_type=F32) + b_ref[...]


def _combine_kernel(dest_ref, ys_ref, gate_ref, x_ref, gmod_ref, o_ref, buf, sem):
    base = pl.program_id(0) * (TM * TOP_K)

    def copy(r, k):
        d = dest_ref[base + r * TOP_K + k]
        return pltpu.make_async_copy(ys_ref.at[pl.ds(d, 1), :], buf.at[k, pl.ds(r, 1), :], sem)

    def issue(r, carry):
        for k in range(TOP_K):
            copy(r, k).start()
        return carry

    def drain(r, carry):
        for k in range(TOP_K):
            copy(r, k).wait()
        return carry

    lax.fori_loop(0, TM, issue, 0)
    lax.fori_loop(0, TM, drain, 0)
    gates = gate_ref[...]
    y = gates[:, 0:1] * buf[0]
    for k in range(1, TOP_K):
        y = y + gates[:, k : k + 1] * buf[k]
    o_ref[...] = x_ref[...] + gmod_ref[...] * y


def _moe_layer(g, layer, x, mod, norm_g, w_router, b_router, w_gate_up, b_gate_up, w_down, b_down):
    m = x.shape[0]
    n_tiles = m // TM
    depth = w_gate_up.shape[0]
    const = lambda shape: pl.BlockSpec(shape, lambda i: (0,) * len(shape))
    tile4 = pl.BlockSpec((TM, TOP_K), lambda i: (i, 0))
    h2, idx, pos, gates, counts = pl.pallas_call(
        _router_kernel,
        grid=(n_tiles,),
        in_specs=[pl.BlockSpec((TM, D), lambda i: (i, 0)), const((1, D)), _mod_spec(g, 3), _mod_spec(g, 4),
                  const((D, N_EXPERTS)), const((1, N_EXPERTS))],
        out_specs=(pl.BlockSpec((TM, D), lambda i: (i, 0)), tile4, tile4, tile4, const((1, N_EXPERTS))),
        out_shape=(jax.ShapeDtypeStruct((m, D), F32), jax.ShapeDtypeStruct((m, TOP_K), I32),
                   jax.ShapeDtypeStruct((m, TOP_K), I32), jax.ShapeDtypeStruct((m, TOP_K), F32),
                   jax.ShapeDtypeStruct((1, N_EXPERTS), F32)),
        scratch_shapes=[pltpu.VMEM((1, N_EXPERTS), F32)],
        compiler_params=_params(("arbitrary",)),
        name="moe_router",
    )(x, norm_g.reshape(1, D), mod, mod, w_router, b_router.reshape(1, N_EXPERTS))

    n_pairs = m * TOP_K
    n_blocks = -(-(n_pairs + N_EXPERTS * (MOE_BM - 1)) // MOE_BM)
    n_rows = n_blocks * MOE_BM
    counts = counts.reshape(N_EXPERTS).astype(I32)
    padded = (counts + MOE_BM - 1) // MOE_BM * MOE_BM
    pad_end = jnp.cumsum(padded)
    pad_start = pad_end - padded
    dest = (pad_start[idx] + pos).reshape(n_pairs)
    blk0 = jnp.arange(n_blocks, dtype=I32) * MOE_BM
    n_active = (pad_end[-1] // MOE_BM).astype(I32)
    blk_row = jnp.minimum(jnp.arange(n_blocks, dtype=I32), n_active - 1)
    blk_exp = jnp.minimum(jnp.searchsorted(pad_end, blk_row * MOE_BM, side="right"), N_EXPERTS - 1).astype(I32)
    first = jnp.concatenate([jnp.ones((1,), I32), (blk_exp[1:] != blk_exp[:-1]).astype(I32)])
    n_valid = jnp.clip(counts[blk_exp] - (blk0 - pad_start[blk_exp]), 0, MOE_BM).astype(I32)
    sched = (blk_exp, blk_row, first, n_valid, n_active.reshape(1))

    xs = pl.pallas_call(
        _dispatch_kernel,
        grid_spec=pltpu.PrefetchScalarGridSpec(
            num_scalar_prefetch=1,
            grid=(n_tiles,),
            in_specs=[pl.BlockSpec((TM, D), lambda i, d: (i, 0))],
            out_specs=pl.BlockSpec(memory_space=pl.ANY),
            scratch_shapes=[pltpu.SemaphoreType.DMA],
        ),
        out_shape=jax.ShapeDtypeStruct((n_rows, D), F32),
        compiler_params=_params(("arbitrary",)),
        name="moe_dispatch",
    )(dest, h2)

    e_tiles = EXPERT_DIM // MOE_BN
    w_spec = lambda off: pl.BlockSpec((None, None, D, MOE_BN), lambda n, b, be, *_: (layer, be[b], 0, n + off))
    bias_spec = lambda off: pl.BlockSpec((None, None, 1, MOE_BN), lambda n, b, be, *_: (layer, be[b], 0, n + off))
    hid = pl.pallas_call(
        _expert_up_kernel,
        grid_spec=pltpu.PrefetchScalarGridSpec(
            num_scalar_prefetch=5,
            grid=(e_tiles, n_blocks),
            in_specs=[pl.BlockSpec((MOE_BM, D), lambda n, b, be, br, *_: (br[b], 0)),
                      w_spec(0), w_spec(e_tiles), bias_spec(0), bias_spec(e_tiles)],
            out_specs=pl.BlockSpec((MOE_BM, MOE_BN), lambda n, b, be, br, *_: (br[b], n)),
            scratch_shapes=[pltpu.VMEM((D, MOE_BN), BF16), pltpu.VMEM((D, MOE_BN), BF16)],
        ),
        out_shape=jax.ShapeDtypeStruct((n_rows, EXPERT_DIM), BF16),
        compiler_params=_params(("arbitrary", "arbitrary")),
        name="moe_up",
    )(*sched, xs, w_gate_up, w_gate_up, b_gate_up.reshape(depth, N_EXPERTS, 1, 2 * EXPERT_DIM),
      b_gate_up.reshape(depth, N_EXPERTS, 1, 2 * EXPERT_DIM))

    ys = pl.pallas_call(
        _expert_down_kernel,
        grid_spec=pltpu.PrefetchScalarGridSpec(
            num_scalar_prefetch=5,
            grid=(D // MOE_BN, n_blocks),
            in_specs=[pl.BlockSpec((MOE_BM, EXPERT_DIM), lambda n, b, be, br, *_: (br[b], 0)),
                      pl.BlockSpec((None, None, EXPERT_DIM, MOE_BN), lambda n, b, be, *_: (layer, be[b], 0, n)),
                      pl.BlockSpec((None, None, 1, MOE_BN), lambda n, b, be, *_: (layer, be[b], 0, n))],
            out_specs=pl.BlockSpec((MOE_BM, MOE_BN), lambda n, b, be, br, *_: (br[b], n)),
            scratch_shapes=[pltpu.VMEM((EXPERT_DIM, MOE_BN), BF16)],
        ),
        out_shape=jax.ShapeDtypeStruct((n_rows, D), F32),
        compiler_params=_params(("arbitrary", "arbitrary")),
        name="moe_down",
    )(*sched, hid, w_down, b_down.reshape(depth, N_EXPERTS, 1, D))

    return pl.pallas_call(
        _combine_kernel,
        grid_spec=pltpu.PrefetchScalarGridSpec(
            num_scalar_prefetch=1,
            grid=(n_tiles,),
            in_specs=[pl.BlockSpec(memory_space=pl.ANY),
                      pl.BlockSpec((TM, TOP_K), lambda i, d: (i, 0)),
                      pl.BlockSpec((TM, D), lambda i, d: (i, 0)),
                      pl.BlockSpec((None, 1, D), lambda i, d: (_mod_row(g, i), 0, 5))],
            out_specs=pl.BlockSpec((TM, D), lambda i, d: (i, 0)),
            scratch_shapes=[pltpu.VMEM((TOP_K, TM, D), F32), pltpu.SemaphoreType.DMA],
        ),
        out_shape=jax.ShapeDtypeStruct((m, D), F32),
        compiler_params=_params(("arbitrary",)),
        name="moe_combine",
    )(dest, ys, gates, x, mod)


def _trunk(g, x_prompt, x_sample, c, c_ctx, cache_mla_ckv, cache_mla_krope, cache_diff_k, cache_diff_v,
           norm1_g, norm2_g, w_mod, b_mod, moe_w_router, moe_b_router, moe_w_gate_up, moe_b_gate_up,
           moe_w_down, moe_b_down, gmlp, mla, diff, pool):
    depth = w_mod.shape[0]
    nc, nl = _ctx_rows(g), _lat_rows(g)
    x = jnp.concatenate([x_prompt.reshape(nc, D), x_sample.reshape(nl, D)], axis=0)
    cond8 = jnp.concatenate([c_ctx.reshape(1, D), c, jnp.zeros((8 - 1 - g.n_lat, D), F32)], axis=0)
    mods = _modulation(cond8, w_mod, b_mod)
    states = {}
    for l in range(depth):
        kind, j = l % 4, l // 4
        mod = mods[l]
        if kind == 0:
            x = _gmlp_layer(g, x, mod, norm1_g[l], *[p[j] for p in gmlp])
        elif kind == 1:
            x, ckv, krope = _mla_layer(g, x, mod, norm1_g[l], cache_mla_ckv[:, j], cache_mla_krope[:, j],
                                       *[p[j] for p in mla])
            states.setdefault("ckv", []).append(ckv.reshape(g.n_ctx, g.ctx_len, MLA_RANK))
            states.setdefault("krope", []).append(krope.reshape(g.n_ctx, g.ctx_len, MLA_ROPE))
        elif kind == 2:
            lam_init = 0.8 - 0.6 * math.exp(-0.3 * l)
            x, dk, dv = _diff_layer(g, x, mod, norm1_g[l], cache_diff_k[:, j], cache_diff_v[:, j],
                                    *[p[j] for p in diff], lam_init)
            shape = (g.n_ctx, g.ctx_len, DIFF_HEADS, 2 * DIFF_DIM)
            states.setdefault("dk", []).append(dk.reshape(shape))
            states.setdefault("dv", []).append(dv.reshape(shape))
        else:
            x = _pool_layer(g, x, mod, norm1_g[l], *[p[j] for p in pool])
        x = _moe_layer(g, l, x, mod, norm2_g[l], moe_w_router[l], moe_b_router[l], moe_w_gate_up, moe_b_gate_up,
                       moe_w_down, moe_b_down)
    y_prompt = x[:nc].reshape(x_prompt.shape)
    y_sample = x[nc:].reshape(x_sample.shape)
    return (y_prompt, y_sample, jnp.stack(states["ckv"], axis=1), jnp.stack(states["krope"], axis=1),
            jnp.stack(states["dk"], axis=1), jnp.stack(states["dv"], axis=1))


def kernel(x_prompt, x_sample, c, c_ctx, cache_mla_ckv, cache_mla_krope, cache_diff_k, cache_diff_v, norm1_g, norm2_g, w_mod, b_mod, moe_w_router, moe_b_router, moe_w_gate_up, moe_b_gate_up, moe_w_down, moe_b_down, gmlp_w_in, gmlp_b_in, gmlp_ln_g, gmlp_ln_b, gmlp_w_s, gmlp_b_s, gmlp_w_out, gmlp_b_out, mla_w_dq, mla_g_qa, mla_w_uq, mla_w_dkv, mla_g_kva, mla_w_kr, mla_w_uk, mla_w_uv, mla_g_q, mla_g_k, mla_w_o, diff_w_qkv, diff_g_q, diff_g_k, diff_lambda, diff_g_sub, diff_w_o, pool_w, pool_b, pool_scale):
    g = Geom(x_prompt.shape[0], x_prompt.shape[1], x_sample.shape[0], x_sample.shape[1], cache_mla_ckv.shape[2])
    gmlp = (gmlp_w_in, gmlp_b_in, gmlp_ln_g, gmlp_ln_b, gmlp_w_s, gmlp_b_s, gmlp_w_out, gmlp_b_out)
    mla = (mla_w_dq, mla_g_qa, mla_w_uq, mla_w_dkv, mla_g_kva, mla_w_kr, mla_w_uk, mla_w_uv, mla_g_q, mla_g_k,
           mla_w_o)
    diff = (diff_w_qkv, diff_g_q, diff_g_k, diff_lambda, diff_g_sub, diff_w_o)
    pool = (pool_w, pool_b, pool_scale)
    return _trunk(g, x_prompt, x_sample, c, c_ctx, cache_mla_ckv, cache_mla_krope, cache_diff_k, cache_diff_v,
                  norm1_g, norm2_g, w_mod, b_mod, moe_w_router, moe_b_router, moe_w_gate_up, moe_b_gate_up,
                  moe_w_down, moe_b_down, gmlp, mla, diff, pool)
```

```python
import collections
import functools
import math

import jax
import jax.numpy as jnp
from jax import lax
from jax.experimental import pallas as pl
from jax.experimental.pallas import tpu as pltpu

F32 = jnp.float32
BF16 = jnp.bfloat16
I32 = jnp.int32

D = 2048
EPS = 1e-6
ROPE_THETA = 10000.0
GRID_W = 64
TM = 256
MM_BM = 512
MM_BN = 1024
LANES = 128
VMEM_LIMIT = 56 * 1024 * 1024

GMLP_CHUNK = 128
GMLP_GROUPS = 16
MLA_HEADS = 16
MLA_RANK = 512
MLA_NOPE = 128
MLA_ROPE = 64
MLA_QK = MLA_NOPE + MLA_ROPE
DIFF_HEADS = 8
DIFF_DIM = 128
POOL_WINDOWS = (2, 4, 8, 16)
POOL_GROUP_DIM = 512
N_EXPERTS = 32
TOP_K = 4
EXPERT_DIM = 2048
SWIGLU_LIMIT = 7.0
SWIGLU_ALPHA = 1.702
MOE_BM = 256
MOE_BN = 1024

Geom = collections.namedtuple("Geom", "n_ctx ctx_len n_lat lat_len past_len")


def _ctx_rows(g):
    return g.n_ctx * g.ctx_len


def _lat_rows(g):
    return g.n_lat * g.lat_len


def _mod_row(g, i, bm=TM):
    ct = _ctx_rows(g) // bm
    return jnp.where(i < ct, 0, 1 + (i - ct) // (g.lat_len // bm))


def _rope_tile(g, i, bm):
    ct = _ctx_rows(g) // bm
    return jnp.where(i < ct, 0, 1 + (i - ct) % (g.lat_len // bm))


def _params(sem):
    return pltpu.CompilerParams(dimension_semantics=sem, vmem_limit_bytes=VMEM_LIMIT)


def _mod_spec(g, k, n_axis=None, bn=D):
    per = D // bn
    if n_axis is None:
        return pl.BlockSpec((None, 1, bn), lambda m: (_mod_row(g, m), 0, k * per))
    return pl.BlockSpec((None, 1, bn), lambda n, m: (_mod_row(g, m, MM_BM), 0, k * per + n))


def _rms(x, gain):
    return x * lax.rsqrt(jnp.mean(x * x, axis=-1, keepdims=True) + EPS) * gain


def _prenorm(x, gain, shift, scale):
    return _rms(x, gain) * (1.0 + scale) + shift


def _modulation_kernel(c_ref, w_ref, b_ref, o_ref):
    c = c_ref[...]
    s = c * (1.0 / (1.0 + jnp.exp(-c)))
    o_ref[...] = jnp.dot(s.astype(BF16), w_ref[...].astype(BF16), preferred_element_type=F32) + b_ref[...]


def _modulation(cond8, w_mod, b_mod):
    depth = w_mod.shape[0]
    bn = 1024
    out = pl.pallas_call(
        _modulation_kernel,
        grid=(depth, 6 * D // bn),
        in_specs=[
            pl.BlockSpec((8, D), lambda l, n: (0, 0)),
            pl.BlockSpec((None, D, bn), lambda l, n: (l, 0, n)),
            pl.BlockSpec((None, 1, bn), lambda l, n: (l, 0, n)),
        ],
        out_specs=pl.BlockSpec((None, 8, bn), lambda l, n: (l, 0, n)),
        out_shape=jax.ShapeDtypeStruct((depth, 8, 6 * D), F32),
        compiler_params=_params(("arbitrary", "arbitrary")),
        name="modulation",
    )(cond8, w_mod, b_mod.reshape(depth, 1, 6 * D))
    return out.reshape(depth, 8, 1, 6 * D)


def _prenorm_kernel(x_ref, g_ref, sh_ref, sc_ref, o_ref):
    o_ref[...] = _prenorm(x_ref[...], g_ref[...], sh_ref[...], sc_ref[...]).astype(o_ref.dtype)


def _prenorm_call(g, x, gain, mod, k_shift):
    m = x.shape[0]
    return pl.pallas_call(
        _prenorm_kernel,
        grid=(m // TM,),
        in_specs=[
            pl.BlockSpec((TM, D), lambda i: (i, 0)),
            pl.BlockSpec((1, D), lambda i: (0, 0)),
            _mod_spec(g, k_shift),
            _mod_spec(g, k_shift + 1),
        ],
        out_specs=pl.BlockSpec((TM, D), lambda i: (i, 0)),
        out_shape=jax.ShapeDtypeStruct((m, D), BF16),
        compiler_params=_params(("arbitrary",)),
        name="prenorm",
    )(x, gain.reshape(1, D), mod, mod)


def _mm_kernel(n_extra, epilogue, x_ref, w_ref, *rest):
    extras, outs, wbf_ref = rest[:n_extra], rest[n_extra:-1], rest[-1]

    @pl.when(pl.program_id(1) == 0)
    def _():
        wbf_ref[...] = w_ref[...].astype(BF16)

    acc = jnp.dot(x_ref[...].astype(BF16), wbf_ref[...], preferred_element_type=F32)
    epilogue(acc, extras, outs)


def _mm(x, w, ncols, bn, epilogue, extras, out_shape, out_specs, name, w_off=0):
    m, k = x.shape
    return pl.pallas_call(
        functools.partial(_mm_kernel, len(extras), epilogue),
        grid=(ncols // bn, m // MM_BM),
        in_specs=[
            pl.BlockSpec((MM_BM, k), lambda n, i: (i, 0)),
            pl.BlockSpec((k, bn), lambda n, i: (0, n + w_off)),
        ]
        + [s for _, s in extras],
        out_specs=out_specs,
        out_shape=out_shape,
        scratch_shapes=[pltpu.VMEM((k, bn), BF16)],
        compiler_params=_params(("arbitrary", "arbitrary")),
        name=name,
    )(x, w, *[a for a, _ in extras])


def _row_spec(bn):
    return pl.BlockSpec((1, bn), lambda n, i: (0, n))


def _tile_spec(bn):
    return pl.BlockSpec((MM_BM, bn), lambda n, i: (i, n))


def _residual_epilogue(acc, extras, outs):
    b_ref, x_ref, gate_ref = extras
    outs[0][...] = x_ref[...] + gate_ref[...] * (acc + b_ref[...])


def _mm_residual(g, h, w, bias, x, mod, k_gate, name):
    m = x.shape[0]
    bn = MM_BN
    extras = [(bias.reshape(1, D), _row_spec(bn)), (x, _tile_spec(bn)), (mod, _mod_spec(g, k_gate, 0, bn))]
    return _mm(h, w, D, bn, _residual_epilogue, extras, jax.ShapeDtypeStruct((m, D), F32), _tile_spec(bn), name)


def _gelu_epilogue(acc, extras, outs):
    z = acc + extras[0][...]
    outs[0][...] = (0.5 * z * (1.0 + lax.erf(z * (2.0 ** -0.5)))).astype(BF16)


def _gmlp_gate_kernel(u_ref, v_ref, lg_ref, lb_ref, ws_ref, bs_ref, o_ref):
    v = v_ref[...].astype(F32)
    mu = jnp.mean(v, axis=-1, keepdims=True)
    vc = v - mu
    var = jnp.mean(vc * vc, axis=-1, keepdims=True)
    vn = (vc * lax.rsqrt(var + EPS) * lg_ref[...] + lb_ref[...]).astype(BF16)
    for grp in range(GMLP_GROUPS):
        cols = slice(grp * LANES, (grp + 1) * LANES)
        w = ws_ref[grp].astype(BF16)
        bias = bs_ref[:, grp : grp + 1]
        for c in range(TM // GMLP_CHUNK):
            rows = slice(c * GMLP_CHUNK, (c + 1) * GMLP_CHUNK)
            vm = jnp.dot(w, vn[rows, cols], preferred_element_type=F32) + bias
            o_ref[rows, cols] = (u_ref[rows, cols].astype(F32) * vm).astype(BF16)


def _gmlp_layer(g, x, mod, norm_g, w_in, b_in, ln_g, ln_b, w_s, b_s, w_out, b_out):
    m = x.shape[0]
    width = D
    h = _prenorm_call(g, x, norm_g, mod, 0)
    bn = MM_BN
    z = _mm(h, w_in, 2 * width, bn, _gelu_epilogue, [(b_in.reshape(1, 2 * width), _row_spec(bn))],
            jax.ShapeDtypeStruct((m, 2 * width), BF16), _tile_spec(bn), "gmlp_in")
    gated = pl.pallas_call(
        _gmlp_gate_kernel,
        grid=(m // TM,),
        in_specs=[
            pl.BlockSpec((TM, width), lambda i: (i, 0)),
            pl.BlockSpec((TM, width), lambda i: (i, 1)),
            pl.BlockSpec((1, width), lambda i: (0, 0)),
            pl.BlockSpec((1, width), lambda i: (0, 0)),
            pl.BlockSpec((GMLP_GROUPS, GMLP_CHUNK, GMLP_CHUNK), lambda i: (0, 0, 0)),
            pl.BlockSpec((GMLP_CHUNK, GMLP_GROUPS), lambda i: (0, 0)),
        ],
        out_specs=pl.BlockSpec((TM, width), lambda i: (i, 0)),
        out_shape=jax.ShapeDtypeStruct((m, width), BF16),
        compiler_params=_params(("arbitrary",)),
        name="gmlp_gate",
    )(z, z, ln_g.reshape(1, width), ln_b.reshape(1, width), w_s, b_s.T)
    return _mm_residual(g, gated, w_out, b_out, x, mod, 2, "gmlp_out")


def _rope_tables(g, d):
    nf = d // 4
    t = jnp.arange(g.lat_len)
    row = (t // GRID_W).astype(F32)
    col = (t % GRID_W).astype(F32)
    inv = ROPE_THETA ** (-jnp.arange(nf, dtype=F32) / nf)
    ang_r = row[:, None] * inv[None, :]
    ang_c = col[:, None] * inv[None, :]
    cos = jnp.concatenate([jnp.cos(ang_r)] * 2 + [jnp.cos(ang_c)] * 2, axis=-1)
    sin = jnp.concatenate([-jnp.sin(ang_r), jnp.sin(ang_r), -jnp.sin(ang_c), jnp.sin(ang_c)], axis=-1)
    reps = LANES // d
    cos = jnp.tile(cos, (1, reps))
    sin = jnp.tile(sin, (1, reps))
    cos = jnp.concatenate([jnp.ones((MM_BM, LANES), F32), cos], axis=0)
    sin = jnp.concatenate([jnp.zeros((MM_BM, LANES), F32), sin], axis=0)
    return cos, sin


def _rope_specs(g):
    return pl.BlockSpec((MM_BM, LANES), lambda n, i: (_rope_tile(g, i, MM_BM), 0))


def _rope128(x, cos, sin, nf):
    lane = lax.broadcasted_iota(I32, x.shape, 1)
    swapped = jnp.where((lane % (2 * nf)) < nf, pltpu.roll(x, LANES - nf, 1), pltpu.roll(x, nf, 1))
    return x * cos + swapped * sin


def _softmax_rows(s):
    s = s - jnp.max(s, axis=-1, keepdims=True)
    p = jnp.exp(s)
    return p / jnp.sum(p, axis=-1, keepdims=True)


def _dot_t(a, b):
    return lax.dot_general(a, b, (((1,), (1,)), ((), ())), preferred_element_type=F32)


def _rmsnorm_epilogue(acc, extras, outs):
    outs[0][...] = _rms(acc, extras[0][...]).astype(outs[0].dtype)


def _plain_epilogue(acc, extras, outs):
    outs[0][...] = acc.astype(outs[0].dtype)


def _pair_select(lane_lo, a, b):
    return jnp.where(lane_lo, a, b)


def _mla_q_epilogue(acc, extras, outs):
    gn_ref, gr_ref, cos_ref, sin_ref = extras
    qn_ref, qr_ref = outs
    nope_w = MLA_HEADS * MLA_NOPE
    lane_lo = lax.broadcasted_iota(I32, (acc.shape[0], LANES), 1) < MLA_ROPE
    cos, sin = cos_ref[...], sin_ref[...]
    for pair in range(MLA_HEADS // 2):
        r = acc[:, nope_w + pair * LANES : nope_w + (pair + 1) * LANES]
        r2 = r * r
        ss_lo = jnp.sum(jnp.where(lane_lo, r2, 0.0), axis=-1, keepdims=True)
        ss_hi = jnp.sum(jnp.where(lane_lo, 0.0, r2), axis=-1, keepdims=True)
        rinv = []
        for j, ss_r in enumerate((ss_lo, ss_hi)):
            h = 2 * pair + j
            qn = acc[:, h * LANES : (h + 1) * LANES]
            ri = lax.rsqrt((jnp.sum(qn * qn, axis=-1, keepdims=True) + ss_r) * (1.0 / MLA_QK) + EPS)
            qn_ref[:, h * LANES : (h + 1) * LANES] = (qn * ri * gn_ref[...]).astype(BF16)
            rinv.append(ri)
        rn = r * _pair_select(lane_lo, rinv[0], rinv[1]) * gr_ref[...]
        qr_ref[:, pair * LANES : (pair + 1) * LANES] = _rope128(rn, cos, sin, MLA_ROPE // 4).astype(BF16)


def _mla_kv_epilogue(acc, extras, outs):
    kr_in_ref, gn_ref, gr_ref, cos_ref, sin_ref = extras
    kn_ref, kr_ref, v_ref = outs
    nope_w = MLA_HEADS * MLA_NOPE
    lane_lo = lax.broadcasted_iota(I32, (acc.shape[0], LANES), 1) < MLA_ROPE
    kr = kr_in_ref[...]
    ss_r = jnp.sum(kr * kr, axis=-1, keepdims=True)
    kr2 = jnp.concatenate([kr, kr], axis=-1) * gr_ref[...]
    kr2 = _rope128(kr2, cos_ref[...], sin_ref[...], MLA_ROPE // 4)
    rinv = []
    for h in range(MLA_HEADS):
        kn = acc[:, h * LANES : (h + 1) * LANES]
        ri = lax.rsqrt((jnp.sum(kn * kn, axis=-1, keepdims=True) + ss_r) * (1.0 / MLA_QK) + EPS)
        kn_ref[:, h * LANES : (h + 1) * LANES] = (kn * ri * gn_ref[...]).astype(BF16)
        rinv.append(ri)
    for pair in range(MLA_HEADS // 2):
        scale = _pair_select(lane_lo, rinv[2 * pair], rinv[2 * pair + 1])
        kr_ref[:, pair * LANES : (pair + 1) * LANES] = (kr2 * scale).astype(BF16)
    v_ref[...] = acc[:, nope_w:].astype(BF16)


def _mla_attn_kernel(qn_ref, qr_ref, kn_ref, kr_ref, v_ref, o_ref):
    scale = MLA_QK ** -0.5
    for h in range(MLA_HEADS):
        cn = slice(h * MLA_NOPE, (h + 1) * MLA_NOPE)
        cr = slice(h * MLA_ROPE, (h + 1) * MLA_ROPE)
        s = _dot_t(qn_ref[:, cn], kn_ref[:, cn]) + _dot_t(qr_ref[:, cr], kr_ref[:, cr])
        p = _softmax_rows(s * scale)
        o_ref[:, cn] = jnp.dot(p.astype(BF16), v_ref[:, cn], preferred_element_type=F32).astype(BF16)


def _attention_call(kernel, q_arrays, kv_arrays, n_seq, q_len, kv_len, q_row0, out_width, extras, name):
    qb = q_len // TM
    q_specs = [pl.BlockSpec((TM, a.shape[1]), lambda s, j: (q_row0 // TM + s * qb + j, 0)) for a in q_arrays]
    kv_specs = [pl.BlockSpec((kv_len, a.shape[1]), lambda s, j: (s, 0)) for a in kv_arrays]
    return pl.pallas_call(
        kernel,
        grid=(n_seq, qb),
        in_specs=q_specs + kv_specs + [s for _, s in extras],
        out_specs=pl.BlockSpec((TM, out_width), lambda s, j: (s * qb + j, 0)),
        out_shape=jax.ShapeDtypeStruct((n_seq * q_len, out_width), BF16),
        compiler_params=_params(("arbitrary", "arbitrary")),
        name=name,
    )(*q_arrays, *kv_arrays, *[a for a, _ in extras])


def _latent_kv(g, own, cache):
    nc = _ctx_rows(g)
    own = own[nc:].reshape(g.n_lat, g.lat_len, own.shape[1])
    cache = cache.reshape(g.n_lat, g.past_len, cache.shape[1])
    return jnp.concatenate([cache, own], axis=1).reshape(g.n_lat * (g.past_len + g.lat_len), own.shape[2])


def _mla_layer(g, x, mod, norm_g, cache_ckv, cache_krope, w_dq, g_qa, w_uq, w_dkv, g_kva, w_kr, w_uk, w_uv,
               g_q, g_k, w_o):
    m = x.shape[0]
    nc, nl = _ctx_rows(g), _lat_rows(g)
    h = _prenorm_call(g, x, norm_g, mod, 0)
    cos, sin = _rope_tables(g, MLA_ROPE)
    rope_extras = [(cos, _rope_specs(g)), (sin, _rope_specs(g))]
    full = lambda w: pl.BlockSpec((1, w), lambda n, i: (0, 0))

    qa = _mm(h, w_dq, MLA_RANK, MLA_RANK, _rmsnorm_epilogue, [(g_qa.reshape(1, MLA_RANK), full(MLA_RANK))],
             jax.ShapeDtypeStruct((m, MLA_RANK), BF16), _tile_spec(MLA_RANK), "mla_dq")
    ckv = _mm(h, w_dkv, MLA_RANK, MLA_RANK, _rmsnorm_epilogue, [(g_kva.reshape(1, MLA_RANK), full(MLA_RANK))],
              jax.ShapeDtypeStruct((m, MLA_RANK), F32), _tile_spec(MLA_RANK), "mla_dkv")
    krope = _mm(h, w_kr, MLA_ROPE, MLA_ROPE, _plain_epilogue, [],
                jax.ShapeDtypeStruct((m, MLA_ROPE), F32), _tile_spec(MLA_ROPE), "mla_kr")

    w_uq3 = w_uq.reshape(MLA_RANK, MLA_HEADS, MLA_QK)
    w_uq_p = jnp.concatenate([w_uq3[:, :, :MLA_NOPE].reshape(MLA_RANK, -1),
                              w_uq3[:, :, MLA_NOPE:].reshape(MLA_RANK, -1)], axis=1)
    qw = w_uq_p.shape[1]
    gq_n = g_q[:MLA_NOPE].reshape(1, MLA_NOPE)
    gq_r = jnp.tile(g_q[MLA_NOPE:], 2).reshape(1, LANES)
    nope_w, rope_w = MLA_HEADS * MLA_NOPE, MLA_HEADS * MLA_ROPE
    qn, qr = _mm(qa, w_uq_p, qw, qw, _mla_q_epilogue,
                 [(gq_n, full(LANES)), (gq_r, full(LANES))] + rope_extras,
                 (jax.ShapeDtypeStruct((m, nope_w), BF16), jax.ShapeDtypeStruct((m, rope_w), BF16)),
                 (pl.BlockSpec((MM_BM, nope_w), lambda n, i: (i, 0)), pl.BlockSpec((MM_BM, rope_w), lambda n, i: (i, 0))),
                 "mla_uq")

    n_cache = g.n_lat * g.past_len
    ckv_all = jnp.concatenate([ckv, cache_ckv.reshape(n_cache, MLA_RANK)], axis=0)
    kr_all = jnp.concatenate([krope, cache_krope.reshape(n_cache, MLA_ROPE)], axis=0)
    w_ukv = jnp.concatenate([w_uk, w_uv], axis=1)
    gk_n = g_k[:MLA_NOPE].reshape(1, MLA_NOPE)
    gk_r = jnp.tile(g_k[MLA_NOPE:], 2).reshape(1, LANES)
    m_all = m + n_cache
    n_tok_tiles = m // MM_BM
    kv_rope = pl.BlockSpec((MM_BM, LANES), lambda n, i: (jnp.where(i < n_tok_tiles, _rope_tile(g, i, MM_BM), 0), 0))
    kn, kr, v = _mm(ckv_all, w_ukv, 2 * nope_w, 2 * nope_w, _mla_kv_epilogue,
                    [(kr_all, pl.BlockSpec((MM_BM, MLA_ROPE), lambda n, i: (i, 0))), (gk_n, full(LANES)),
                     (gk_r, full(LANES)), (cos, kv_rope), (sin, kv_rope)],
                    (jax.ShapeDtypeStruct((m_all, nope_w), BF16), jax.ShapeDtypeStruct((m_all, rope_w), BF16),
                     jax.ShapeDtypeStruct((m_all, nope_w), BF16)),
                    (pl.BlockSpec((MM_BM, nope_w), lambda n, i: (i, 0)), pl.BlockSpec((MM_BM, rope_w), lambda n, i: (i, 0)),
                     pl.BlockSpec((MM_BM, nope_w), lambda n, i: (i, 0))),
                    "mla_ukv")

    o_ctx = _attention_call(_mla_attn_kernel, [qn, qr], [kn, kr, v], g.n_ctx, g.ctx_len, g.ctx_len, 0,
                            nope_w, [], "mla_attn_ctx")
    lat_kv = [_latent_kv(g, a[:m], a[m:]) for a in (kn, kr, v)]
    o_lat = _attention_call(_mla_attn_kernel, [qn, qr], lat_kv, g.n_lat, g.lat_len, g.past_len + g.lat_len, nc,
                            nope_w, [], "mla_attn_lat")
    o = jnp.concatenate([o_ctx, o_lat], axis=0)
    x = _mm_residual(g, o, w_o, jnp.zeros((D,), F32), x, mod, 2, "mla_out")
    return x, ckv[:nc], krope[:nc]


def _diff_qk_epilogue(acc, extras, outs):
    g_ref, cos_ref, sin_ref = extras
    cos, sin = cos_ref[...], sin_ref[...]
    for j in range(acc.shape[1] // LANES):
        cols = slice(j * LANES, (j + 1) * LANES)
        y = _rms(acc[:, cols], g_ref[...])
        if len(outs) == 2:
            outs[1][:, cols] = y
        outs[0][:, cols] = _rope128(y, cos, sin, DIFF_DIM // 4).astype(BF16)


def _diff_v_epilogue(acc, extras, outs):
    outs[0][...] = acc.astype(BF16)
    outs[1][...] = acc


def _diff_attn_kernel(lam_init, q_ref, k_ref, v_ref, lam_ref, gs_ref, o_ref):
    lam = lam_ref[...]
    lam_full = (jnp.exp(jnp.sum(lam[0:1] * lam[1:2], axis=-1, keepdims=True))
                - jnp.exp(jnp.sum(lam[2:3] * lam[3:4], axis=-1, keepdims=True)) + lam_init)
    scale = DIFF_DIM ** -0.5
    for h in range(DIFF_HEADS):
        c0 = slice(2 * h * DIFF_DIM, (2 * h + 1) * DIFF_DIM)
        c1 = slice((2 * h + 1) * DIFF_DIM, (2 * h + 2) * DIFF_DIM)
        cv = slice(2 * h * DIFF_DIM, (2 * h + 2) * DIFF_DIM)
        p0 = _softmax_rows(_dot_t(q_ref[:, c0], k_ref[:, c0]) * scale)
        p1 = _softmax_rows(_dot_t(q_ref[:, c1], k_ref[:, c1]) * scale)
        p = p0 - lam_full * p1
        o = jnp.dot(p.astype(BF16), v_ref[:, cv], preferred_element_type=F32)
        o_ref[:, cv] = (_rms(o, gs_ref[...]) * (1.0 - lam_init)).astype(BF16)


def _diff_layer(g, x, mod, norm_g, cache_k, cache_v, w_qkv, g_q, g_k, lam, g_sub, w_o, lam_init):
    m = x.shape[0]
    nc = _ctx_rows(g)
    h = _prenorm_call(g, x, norm_g, mod, 0)
    cos, sin = _rope_tables(g, DIFF_DIM)
    rope_extras = [(cos, _rope_specs(g)), (sin, _rope_specs(g))]
    bn = MM_BN
    gain = lambda a: (a.reshape(1, DIFF_DIM), pl.BlockSpec((1, DIFF_DIM), lambda n, i: (0, 0)))
    (q,) = _mm(h, w_qkv, D, bn, _diff_qk_epilogue, [gain(g_q)] + rope_extras,
               (jax.ShapeDtypeStruct((m, D), BF16),), (_tile_spec(bn),), "diff_q")
    k, k_state = _mm(h, w_qkv, D, bn, _diff_qk_epilogue, [gain(g_k)] + rope_extras,
                     (jax.ShapeDtypeStruct((m, D), BF16), jax.ShapeDtypeStruct((m, D), F32)),
                     (_tile_spec(bn), _tile_spec(bn)), "diff_k", w_off=D // bn)
    v, v_state = _mm(h, w_qkv, D, bn, _diff_v_epilogue, [],
                     (jax.ShapeDtypeStruct((m, D), BF16), jax.ShapeDtypeStruct((m, D), F32)),
                     (_tile_spec(bn), _tile_spec(bn)), "diff_v", w_off=2 * D // bn)
    extras = [(lam, pl.BlockSpec((4, DIFF_DIM), lambda s, j: (0, 0))),
              (g_sub.reshape(1, 2 * DIFF_DIM), pl.BlockSpec((1, 2 * DIFF_DIM), lambda s, j: (0, 0)))]
    kern = functools.partial(_diff_attn_kernel, lam_init)
    o_ctx = _attention_call(kern, [q], [k, v], g.n_ctx, g.ctx_len, g.ctx_len, 0, D, extras, "diff_attn_ctx")
    n_cache = g.n_lat * g.past_len
    k_lat = _latent_kv(g, k, cache_k.reshape(n_cache, D).astype(BF16))
    v_lat = _latent_kv(g, v, cache_v.reshape(n_cache, D).astype(BF16))
    o_lat = _attention_call(kern, [q], [k_lat, v_lat], g.n_lat, g.lat_len, g.past_len + g.lat_len, nc, D,
                            extras, "diff_attn_lat")
    o = jnp.concatenate([o_ctx, o_lat], axis=0)
    x = _mm_residual(g, o, w_o, jnp.zeros((D,), F32), x, mod, 2, "diff_out")
    return x, k_state[:nc], v_state[:nc]


def _pool_kernel(h_ref, x_ref, gate_ref, a_ref, ic_ref, w_ref, b_ref, ps_ref, o_ref):
    hb = h_ref[...]
    win_sum = jnp.dot(a_ref[...], hb, preferred_element_type=F32)
    d = win_sum * ic_ref[...] - hb.astype(F32)
    y = jnp.dot(d.astype(BF16), w_ref[...].astype(BF16), preferred_element_type=F32) + b_ref[...]
    o_ref[...] = x_ref[...] + gate_ref[...] * (y * ps_ref[...])


def _pool_stream(h, x, n_seq, seq_len, row0, mod_row0, mod_rows_per_seq, mod, w, b, scale):
    t = jnp.arange(seq_len)
    bands, inv_counts = [], []
    for win in POOL_WINDOWS:
        lo = jnp.clip(t - win // 2, 0, seq_len)
        hi = jnp.clip(t + win // 2, 0, seq_len)
        bands.append(((t[None, :] >= lo[:, None]) & (t[None, :] < hi[:, None])).astype(BF16))
        inv_counts.append((1.0 / (hi - lo).astype(F32))[:, None])
    band = jnp.stack(bands)
    inv_count = jnp.stack(inv_counts)
    gd = POOL_GROUP_DIM
    seq0 = row0 // seq_len
    per = D // gd
    return pl.pallas_call(
        _pool_kernel,
        grid=(n_seq, len(POOL_WINDOWS)),
        in_specs=[
            pl.BlockSpec((seq_len, gd), lambda s, gi: (seq0 + s, gi)),
            pl.BlockSpec((seq_len, gd), lambda s, gi: (seq0 + s, gi)),
            pl.BlockSpec((None, 1, gd), lambda s, gi: (mod_row0 + s * mod_rows_per_seq, 0, 2 * per + gi)),
            pl.BlockSpec((None, seq_len, seq_len), lambda s, gi: (gi, 0, 0)),
            pl.BlockSpec((None, seq_len, 1), lambda s, gi: (gi, 0, 0)),
            pl.BlockSpec((None, gd, gd), lambda s, gi: (gi, 0, 0)),
            pl.BlockSpec((1, gd), lambda s, gi: (0, gi)),
            pl.BlockSpec((1, gd), lambda s, gi: (0, gi)),
        ],
        out_specs=pl.BlockSpec((seq_len, gd), lambda s, gi: (s, gi)),
        out_shape=jax.ShapeDtypeStruct((n_seq * seq_len, D), F32),
        compiler_params=_params(("arbitrary", "arbitrary")),
        name="pool",
    )(h, x, mod, band, inv_count, w, b.reshape(1, D), scale.reshape(1, D))


def _pool_layer(g, x, mod, norm_g, w, b, scale):
    nc = _ctx_rows(g)
    h = _prenorm_call(g, x, norm_g, mod, 0)
    xc = _pool_stream(h, x, g.n_ctx, g.ctx_len, 0, 0, 0, mod, w, b, scale)
    xl = _pool_stream(h, x, g.n_lat, g.lat_len, nc, 1, 1, mod, w, b, scale)
    return jnp.concatenate([xc, xl], axis=0)


def _split_bf16(a):
    hi = a.astype(BF16)
    return hi, (a - hi.astype(F32)).astype(BF16)


def _pack_bf16_pairs(h):
    bits = lax.bitcast_convert_type(h.astype(BF16).astype(F32), jnp.uint32)
    half = h.shape[1] // 2
    return (bits[:, :half] >> 16) | bits[:, half:]


def _unpack_bf16_pairs(w):
    lo = lax.bitcast_convert_type(w << 16, F32)
    hi = lax.bitcast_convert_type(w & jnp.uint32(0xFFFF0000), F32)
    return jnp.concatenate([lo, hi], axis=1).astype(BF16)


def _router_kernel(x_ref, g_ref, sh_ref, sc_ref, wr_ref, br_ref, h_ref, idx_ref, pos_ref, gate_ref, cnt_ref,
                   carry_ref):
    @pl.when(pl.program_id(0) == 0)
    def _():
        carry_ref[...] = jnp.zeros_like(carry_ref)

    h = _prenorm(x_ref[...], g_ref[...], sh_ref[...], sc_ref[...])
    h_ref[...] = _pack_bf16_pairs(h)
    h_hi, h_lo = _split_bf16(h)
    w_hi, w_lo = _split_bf16(wr_ref[...])
    dot = functools.partial(jnp.dot, preferred_element_type=F32)
    logits = dot(h_hi, w_hi) + dot(h_hi, w_lo) + dot(h_lo, w_hi) + br_ref[...]

    lane = lax.broadcasted_iota(I32, logits.shape, 1).astype(F32)
    vals, hots = [], []
    for k in range(TOP_K):
        top = jnp.max(logits, axis=-1, keepdims=True)
        sel = jnp.min(jnp.where(logits == top, lane, float(N_EXPERTS)), axis=-1, keepdims=True)
        hot = lane == sel
        idx_ref[:, k : k + 1] = sel.astype(I32)
        vals.append(top)
        hots.append(hot)
        logits = jnp.where(hot, -jnp.inf, logits)
    exps = [jnp.exp(v - vals[0]) for v in vals]
    denom = exps[0] + exps[1] + exps[2] + exps[3]
    for k in range(TOP_K):
        gate_ref[:, k : k + 1] = exps[k] / denom

    hot_all = sum(jnp.where(hot, 1.0, 0.0) for hot in hots)
    r = lax.broadcasted_iota(I32, (TM, TM), 0)
    c = lax.broadcasted_iota(I32, (TM, TM), 1)
    earlier = jnp.where(r > c, 1.0, 0.0).astype(BF16)
    base = dot(earlier, hot_all.astype(BF16)) + carry_ref[...]
    for k in range(TOP_K):
        pos_ref[:, k : k + 1] = jnp.sum(jnp.where(hots[k], base, 0.0), axis=-1, keepdims=True).astype(I32)
    carry_ref[...] = carry_ref[...] + jnp.sum(hot_all, axis=0, keepdims=True)
    cnt_ref[...] = carry_ref[...]


ROW_DMA_UNROLL = 8
PAIRS_PER_TILE = TM * TOP_K


def _dispatch_kernel(dest_ref, h_ref, xs_ref, sem):
    base = pl.program_id(0) * PAIRS_PER_TILE

    def issue(r, carry):
        for k in range(TOP_K):
            d = dest_ref[base + r * TOP_K + k]
            pltpu.make_async_copy(h_ref.at[pl.ds(r, 1), :], xs_ref.at[pl.ds(d, 1), :], sem).start()
        return carry

    lax.fori_loop(0, TM, issue, 0, unroll=ROW_DMA_UNROLL)
    rows = xs_ref.at[pl.ds(0, PAIRS_PER_TILE), :]
    pltpu.make_async_copy(rows, rows, sem).wait()


def _expert_mm_kernel(layer, col_tiles, compute, be_ref, br_ref, first_ref, nv_ref, na_ref, run_ref, nxt_ref,
                      nruns_ref, x_ref, w_hbm, *rest):
    n_w = len(col_tiles)
    bias_refs, o_ref, (wbuf, wbf, sems) = rest[:n_w], rest[n_w], rest[n_w + 1 :]
    n, b = pl.program_id(0), pl.program_id(1)
    active = b < na_ref[0]

    def w_copy(slot, nn, e, j):
        col = pl.multiple_of((col_tiles[j] + nn) * MOE_BN, MOE_BN)
        return pltpu.make_async_copy(w_hbm.at[layer, e, :, pl.ds(col, MOE_BN)], wbuf.at[slot, j], sems.at[slot])

    @pl.when(jnp.logical_and(active, first_ref[b] == 1))
    def _():
        visit = n * nruns_ref[0] + run_ref[b]
        slot = visit % 2

        @pl.when(visit == 0)
        def _():
            for j in range(n_w):
                w_copy(slot, n, be_ref[b], j).start()

        for j in range(n_w):
            w_copy(slot, n, be_ref[b], j).wait()
        nxt = nxt_ref[b]

        @pl.when(nxt >= 0)
        def _():
            for j in range(n_w):
                w_copy(1 - slot, n, nxt, j).start()

        @pl.when(jnp.logical_and(nxt < 0, n + 1 < pl.num_programs(0)))
        def _():
            for j in range(n_w):
                w_copy(1 - slot, n + 1, be_ref[0], j).start()

        for j in range(n_w):
            wbf[j] = wbuf[slot, j].astype(BF16)

    @pl.when(active)
    def _():
        compute(x_ref, wbf, bias_refs, o_ref, nv_ref[b])


def _expert_up_compute(x_ref, wbf, bias_refs, o_ref, n_valid):
    row = lax.broadcasted_iota(I32, (MOE_BM, 1), 0)
    x = _unpack_bf16_pairs(jnp.where(row < n_valid, x_ref[...], jnp.uint32(0)))
    gate = jnp.dot(x, wbf[0], preferred_element_type=F32) + bias_refs[0][...]
    up = jnp.dot(x, wbf[1], preferred_element_type=F32) + bias_refs[1][...]
    gate = jnp.minimum(gate, SWIGLU_LIMIT)
    up = jnp.clip(up, -SWIGLU_LIMIT, SWIGLU_LIMIT)
    glu = gate * (1.0 / (1.0 + jnp.exp(-SWIGLU_ALPHA * gate)))
    o_ref[...] = ((up + 1.0) * glu).astype(BF16)


def _expert_down_compute(h_ref, wbf, bias_refs, o_ref, n_valid):
    o_ref[...] = jnp.dot(h_ref[...], wbf[0], preferred_element_type=F32) + bias_refs[0][...]


def _expert_mm(layer, sched, n_blocks, x, w, bias4, col_tiles, n_col_tiles, compute, out_dtype, name):
    k = w.shape[2]
    n_w = len(col_tiles)
    n_rows = n_blocks * MOE_BM
    bias_spec = lambda off: pl.BlockSpec((None, None, 1, MOE_BN), lambda n, b, be, *_: (layer, be[b], 0, n + off))
    return pl.pallas_call(
        functools.partial(_expert_mm_kernel, layer, col_tiles, compute),
        grid_spec=pltpu.PrefetchScalarGridSpec(
            num_scalar_prefetch=len(sched),
            grid=(n_col_tiles, n_blocks),
            in_specs=[pl.BlockSpec((MOE_BM, x.shape[1]), lambda n, b, be, br, *_: (br[b], 0)),
                      pl.BlockSpec(memory_space=pl.ANY)] + [bias_spec(off) for off in col_tiles],
            out_specs=pl.BlockSpec((MOE_BM, MOE_BN), lambda n, b, be, br, *_: (br[b], n)),
            scratch_shapes=[pltpu.VMEM((2, n_w, k, MOE_BN), F32), pltpu.VMEM((n_w, k, MOE_BN), BF16),
                            pltpu.SemaphoreType.DMA((2,))],
        ),
        out_shape=jax.ShapeDtypeStruct((n_rows, n_col_tiles * MOE_BN), out_dtype),
        compiler_params=_params(("arbitrary", "arbitrary")),
        name=name,
    )(*sched, x, w, *([bias4] * n_w))


def _combine_kernel(dest_ref, ys_ref, gate_ref, x_ref, gmod_ref, o_ref, buf, sems):
    i = pl.program_id(0)

    def issue_tile(tile, slot):
        base = tile * PAIRS_PER_TILE

        def issue(r, carry):
            for k in range(TOP_K):
                d = dest_ref[base + r * TOP_K + k]
                pltpu.make_async_copy(ys_ref.at[pl.ds(d, 1), :], buf.at[slot, pl.ds(k * TM + r, 1), :],
                                      sems.at[slot]).start()
            return carry

        lax.fori_loop(0, TM, issue, 0, unroll=ROW_DMA_UNROLL)

    @pl.when(i == 0)
    def _():
        issue_tile(0, 0)

    slot = i % 2

    @pl.when(i + 1 < pl.num_programs(0))
    def _():
        issue_tile(i + 1, 1 - slot)

    pltpu.make_async_copy(ys_ref.at[pl.ds(0, PAIRS_PER_TILE), :], buf.at[slot], sems.at[slot]).wait()
    gates = gate_ref[...]
    y = gates[:, 0:1] * buf[slot, 0:TM, :]
    for k in range(1, TOP_K):
        y = y + gates[:, k : k + 1] * buf[slot, k * TM : (k + 1) * TM, :]
    o_ref[...] = x_ref[...] + gmod_ref[...] * y


def _moe_layer(g, layer, x, mod, norm_g, w_router, b_router, w_gate_up, b_gate_up, w_down, b_down):
    m = x.shape[0]
    n_tiles = m // TM
    depth = w_gate_up.shape[0]
    const = lambda shape: pl.BlockSpec(shape, lambda i: (0,) * len(shape))
    tile4 = pl.BlockSpec((TM, TOP_K), lambda i: (i, 0))
    h2, idx, pos, gates, counts = pl.pallas_call(
        _router_kernel,
        grid=(n_tiles,),
        in_specs=[pl.BlockSpec((TM, D), lambda i: (i, 0)), const((1, D)), _mod_spec(g, 3), _mod_spec(g, 4),
                  const((D, N_EXPERTS)), const((1, N_EXPERTS))],
        out_specs=(pl.BlockSpec((TM, D // 2), lambda i: (i, 0)), tile4, tile4, tile4, const((1, N_EXPERTS))),
        out_shape=(jax.ShapeDtypeStruct((m, D // 2), jnp.uint32), jax.ShapeDtypeStruct((m, TOP_K), I32),
                   jax.ShapeDtypeStruct((m, TOP_K), I32), jax.ShapeDtypeStruct((m, TOP_K), F32),
                   jax.ShapeDtypeStruct((1, N_EXPERTS), F32)),
        scratch_shapes=[pltpu.VMEM((1, N_EXPERTS), F32)],
        compiler_params=_params(("arbitrary",)),
        name="moe_router",
    )(x, norm_g.reshape(1, D), mod, mod, w_router, b_router.reshape(1, N_EXPERTS))

    n_pairs = m * TOP_K
    n_blocks = -(-(n_pairs + N_EXPERTS * (MOE_BM - 1)) // MOE_BM)
    n_rows = n_blocks * MOE_BM
    counts = counts.reshape(N_EXPERTS).astype(I32)
    padded = (counts + MOE_BM - 1) // MOE_BM * MOE_BM
    pad_end = jnp.cumsum(padded)
    pad_start = pad_end - padded
    dest = (pad_start[idx] + pos).reshape(n_pairs)
    blk = jnp.arange(n_blocks, dtype=I32)
    n_active = (pad_end[-1] // MOE_BM).astype(I32)
    blk_row = jnp.minimum(blk, n_active - 1)
    blk_exp = jnp.minimum(jnp.sum(pad_end[None, :] <= (blk_row * MOE_BM)[:, None], axis=1), N_EXPERTS - 1).astype(I32)
    first = jnp.concatenate([jnp.ones((1,), I32), (blk_exp[1:] != blk_exp[:-1]).astype(I32)])
    n_valid = jnp.clip(counts[blk_exp] - (blk * MOE_BM - pad_start[blk_exp]), 0, MOE_BM).astype(I32)
    run_idx = jnp.cumsum(first) - 1
    n_runs = jnp.sum(first).astype(I32)
    used = jnp.concatenate([jnp.nonzero(counts > 0, size=N_EXPERTS, fill_value=-1)[0].astype(I32),
                            jnp.full((1,), -1, I32)])
    nxt_exp = used[run_idx + 1]
    sched = (blk_exp, blk_row, first, n_valid, n_active.reshape(1), run_idx.astype(I32), nxt_exp, n_runs.reshape(1))

    xs = pl.pallas_call(
        _dispatch_kernel,
        grid_spec=pltpu.PrefetchScalarGridSpec(
            num_scalar_prefetch=1,
            grid=(n_tiles,),
            in_specs=[pl.BlockSpec((TM, D // 2), lambda i, d: (i, 0))],
            out_specs=pl.BlockSpec(memory_space=pl.ANY),
            scratch_shapes=[pltpu.SemaphoreType.DMA],
        ),
        out_shape=jax.ShapeDtypeStruct((n_rows, D // 2), jnp.uint32),
        compiler_params=_params(("arbitrary",)),
        name="moe_dispatch",
    )(dest, h2)

    e_tiles = EXPERT_DIM // MOE_BN
    hid = _expert_mm(layer, sched, n_blocks, xs, w_gate_up, b_gate_up.reshape(depth, N_EXPERTS, 1, 2 * EXPERT_DIM),
                     (0, e_tiles), e_tiles, _expert_up_compute, BF16, "moe_up")
    ys = _expert_mm(layer, sched, n_blocks, hid, w_down, b_down.reshape(depth, N_EXPERTS, 1, D),
                    (0,), D // MOE_BN, _expert_down_compute, F32, "moe_down")

    return pl.pallas_call(
        _combine_kernel,
        grid_spec=pltpu.PrefetchScalarGridSpec(
            num_scalar_prefetch=1,
            grid=(n_tiles,),
            in_specs=[pl.BlockSpec(memory_space=pl.ANY),
                      pl.BlockSpec((TM, TOP_K), lambda i, d: (i, 0)),
                      pl.BlockSpec((TM, D), lambda i, d: (i, 0)),
                      pl.BlockSpec((None, 1, D), lambda i, d: (_mod_row(g, i), 0, 5))],
            out_specs=pl.BlockSpec((TM, D), lambda i, d: (i, 0)),
            scratch_shapes=[pltpu.VMEM((2, PAIRS_PER_TILE, D), F32), pltpu.SemaphoreType.DMA((2,))],
        ),
        out_shape=jax.ShapeDtypeStruct((m, D), F32),
        compiler_params=_params(("arbitrary",)),
        name="moe_combine",
    )(dest, ys, gates, x, mod)


def _trunk(g, x_prompt, x_sample, c, c_ctx, cache_mla_ckv, cache_mla_krope, cache_diff_k, cache_diff_v,
           norm1_g, norm2_g, w_mod, b_mod, moe_w_router, moe_b_router, moe_w_gate_up, moe_b_gate_up,
           moe_w_down, moe_b_down, gmlp, mla, diff, pool):
    depth = w_mod.shape[0]
    nc, nl = _ctx_rows(g), _lat_rows(g)
    x = jnp.concatenate([x_prompt.reshape(nc, D), x_sample.reshape(nl, D)], axis=0)
    cond8 = jnp.concatenate([c_ctx.reshape(1, D), c, jnp.zeros((8 - 1 - g.n_lat, D), F32)], axis=0)
    mods = _modulation(cond8, w_mod, b_mod)
    states = {}
    for l in range(depth):
        kind, j = l % 4, l // 4
        mod = mods[l]
        if kind == 0:
            x = _gmlp_layer(g, x, mod, norm1_g[l], *[p[j] for p in gmlp])
        elif kind == 1:
            x, ckv, krope = _mla_layer(g, x, mod, norm1_g[l], cache_mla_ckv[:, j], cache_mla_krope[:, j],
                                       *[p[j] for p in mla])
            states.setdefault("ckv", []).append(ckv.reshape(g.n_ctx, g.ctx_len, MLA_RANK))
            states.setdefault("krope", []).append(krope.reshape(g.n_ctx, g.ctx_len, MLA_ROPE))
        elif kind == 2:
            lam_init = 0.8 - 0.6 * math.exp(-0.3 * l)
            x, dk, dv = _diff_layer(g, x, mod, norm1_g[l], cache_diff_k[:, j], cache_diff_v[:, j],
                                    *[p[j] for p in diff], lam_init)
            shape = (g.n_ctx, g.ctx_len, DIFF_HEADS, 2 * DIFF_DIM)
            states.setdefault("dk", []).append(dk.reshape(shape))
            states.setdefault("dv", []).append(dv.reshape(shape))
        else:
            x = _pool_layer(g, x, mod, norm1_g[l], *[p[j] for p in pool])
        x = _moe_layer(g, l, x, mod, norm2_g[l], moe_w_router[l], moe_b_router[l], moe_w_gate_up, moe_b_gate_up,
                       moe_w_down, moe_b_down)
    y_prompt = x[:nc].reshape(x_prompt.shape)
    y_sample = x[nc:].reshape(x_sample.shape)
    return (y_prompt, y_sample, jnp.stack(states["ckv"], axis=1), jnp.stack(states["krope"], axis=1),
            jnp.stack(states["dk"], axis=1), jnp.stack(states["dv"], axis=1))


def kernel(x_prompt, x_sample, c, c_ctx, cache_mla_ckv, cache_mla_krope, cache_diff_k, cache_diff_v, norm1_g, norm2_g, w_mod, b_mod, moe_w_router, moe_b_router, moe_w_gate_up, moe_b_gate_up, moe_w_down, moe_b_down, gmlp_w_in, gmlp_b_in, gmlp_ln_g, gmlp_ln_b, gmlp_w_s, gmlp_b_s, gmlp_w_out, gmlp_b_out, mla_w_dq, mla_g_qa, mla_w_uq, mla_w_dkv, mla_g_kva, mla_w_kr, mla_w_uk, mla_w_uv, mla_g_q, mla_g_k, mla_w_o, diff_w_qkv, diff_g_q, diff_g_k, diff_lambda, diff_g_sub, diff_w_o, pool_w, pool_b, pool_scale):
    g = Geom(x_prompt.shape[0], x_prompt.shape[1], x_sample.shape[0], x_sample.shape[1], cache_mla_ckv.shape[2])
    gmlp = (gmlp_w_in, gmlp_b_in, gmlp_ln_g, gmlp_ln_b, gmlp_w_s, gmlp_b_s, gmlp_w_out, gmlp_b_out)
    mla = (mla_w_dq, mla_g_qa, mla_w_uq, mla_w_dkv, mla_g_kva, mla_w_kr, mla_w_uk, mla_w_uv, mla_g_q, mla_g_k,
           mla_w_o)
    diff = (diff_w_qkv, diff_g_q, diff_g_k, diff_lambda, diff_g_sub, diff_w_o)
    pool = (pool_w, pool_b, pool_scale)
    return _trunk(g, x_prompt, x_sample, c, c_ctx, cache_mla_ckv, cache_mla_krope, cache_diff_k, cache_diff_v,
                  norm1_g, norm2_g, w_mod, b_mod, moe_w_router, moe_b_router, moe_w_gate_up, moe_b_gate_up,
                  moe_w_down, moe_b_down, gmlp, mla, diff, pool)
```

```python
import collections
import functools
import math

import jax
import jax.numpy as jnp
from jax import lax
from jax.experimental import pallas as pl
from jax.experimental.pallas import tpu as pltpu

F32 = jnp.float32
BF16 = jnp.bfloat16
I32 = jnp.int32

D = 2048
EPS = 1e-6
ROPE_THETA = 10000.0
GRID_W = 64
TM = 256
MM_BM = 512
MM_BN = 1024
LANES = 128
VMEM_LIMIT = 56 * 1024 * 1024

GMLP_CHUNK = 128
GMLP_GROUPS = 16
MLA_HEADS = 16
MLA_RANK = 512
MLA_NOPE = 128
MLA_ROPE = 64
MLA_QK = MLA_NOPE + MLA_ROPE
DIFF_HEADS = 8
DIFF_DIM = 128
POOL_WINDOWS = (2, 4, 8, 16)
POOL_GROUP_DIM = 512
N_EXPERTS = 32
TOP_K = 4
EXPERT_DIM = 2048
SWIGLU_LIMIT = 7.0
SWIGLU_ALPHA = 1.702
MOE_BM = 256
MOE_VARIANTS = ((256, 0, 1), (256, 1, 1), (256, 0, 4), (256, 1, 4))
MOE_BN = 1024

Geom = collections.namedtuple("Geom", "n_ctx ctx_len n_lat lat_len past_len")


def _ctx_rows(g):
    return g.n_ctx * g.ctx_len


def _lat_rows(g):
    return g.n_lat * g.lat_len


def _mod_row(g, i, bm=TM):
    ct = _ctx_rows(g) // bm
    return jnp.where(i < ct, 0, 1 + (i - ct) // (g.lat_len // bm))


def _rope_tile(g, i, bm):
    ct = _ctx_rows(g) // bm
    return jnp.where(i < ct, 0, 1 + (i - ct) % (g.lat_len // bm))


def _params(sem):
    return pltpu.CompilerParams(dimension_semantics=sem, vmem_limit_bytes=VMEM_LIMIT)


def _mod_spec(g, k, n_axis=None, bn=D):
    per = D // bn
    if n_axis is None:
        return pl.BlockSpec((None, 1, bn), lambda m: (_mod_row(g, m), 0, k * per))
    return pl.BlockSpec((None, 1, bn), lambda n, m: (_mod_row(g, m, MM_BM), 0, k * per + n))


def _rms(x, gain):
    return x * lax.rsqrt(jnp.mean(x * x, axis=-1, keepdims=True) + EPS) * gain


def _prenorm(x, gain, shift, scale):
    return _rms(x, gain) * (1.0 + scale) + shift


def _modulation_kernel(c_ref, w_ref, b_ref, o_ref):
    c = c_ref[...]
    s = c * (1.0 / (1.0 + jnp.exp(-c)))
    o_ref[...] = jnp.dot(s.astype(BF16), w_ref[...].astype(BF16), preferred_element_type=F32) + b_ref[...]


def _modulation(cond8, w_mod, b_mod):
    depth = w_mod.shape[0]
    bn = 1024
    out = pl.pallas_call(
        _modulation_kernel,
        grid=(depth, 6 * D // bn),
        in_specs=[
            pl.BlockSpec((8, D), lambda l, n: (0, 0)),
            pl.BlockSpec((None, D, bn), lambda l, n: (l, 0, n)),
            pl.BlockSpec((None, 1, bn), lambda l, n: (l, 0, n)),
        ],
        out_specs=pl.BlockSpec((None, 8, bn), lambda l, n: (l, 0, n)),
        out_shape=jax.ShapeDtypeStruct((depth, 8, 6 * D), F32),
        compiler_params=_params(("arbitrary", "arbitrary")),
        name="modulation",
    )(cond8, w_mod, b_mod.reshape(depth, 1, 6 * D))
    return out.reshape(depth, 8, 1, 6 * D)


def _prenorm_kernel(x_ref, g_ref, sh_ref, sc_ref, o_ref):
    o_ref[...] = _prenorm(x_ref[...], g_ref[...], sh_ref[...], sc_ref[...]).astype(o_ref.dtype)


def _prenorm_call(g, x, gain, mod, k_shift):
    m = x.shape[0]
    return pl.pallas_call(
        _prenorm_kernel,
        grid=(m // TM,),
        in_specs=[
            pl.BlockSpec((TM, D), lambda i: (i, 0)),
            pl.BlockSpec((1, D), lambda i: (0, 0)),
            _mod_spec(g, k_shift),
            _mod_spec(g, k_shift + 1),
        ],
        out_specs=pl.BlockSpec((TM, D), lambda i: (i, 0)),
        out_shape=jax.ShapeDtypeStruct((m, D), BF16),
        compiler_params=_params(("arbitrary",)),
        name="prenorm",
    )(x, gain.reshape(1, D), mod, mod)


def _mm_kernel(n_extra, epilogue, x_ref, w_ref, *rest):
    extras, outs, wbf_ref = rest[:n_extra], rest[n_extra:-1], rest[-1]

    @pl.when(pl.program_id(1) == 0)
    def _():
        wbf_ref[...] = w_ref[...].astype(BF16)

    acc = jnp.dot(x_ref[...].astype(BF16), wbf_ref[...], preferred_element_type=F32)
    epilogue(acc, extras, outs)


def _mm(x, w, ncols, bn, epilogue, extras, out_shape, out_specs, name, w_off=0):
    m, k = x.shape
    return pl.pallas_call(
        functools.partial(_mm_kernel, len(extras), epilogue),
        grid=(ncols // bn, m // MM_BM),
        in_specs=[
            pl.BlockSpec((MM_BM, k), lambda n, i: (i, 0)),
            pl.BlockSpec((k, bn), lambda n, i: (0, n + w_off)),
        ]
        + [s for _, s in extras],
        out_specs=out_specs,
        out_shape=out_shape,
        scratch_shapes=[pltpu.VMEM((k, bn), BF16)],
        compiler_params=_params(("arbitrary", "arbitrary")),
        name=name,
    )(x, w, *[a for a, _ in extras])


def _row_spec(bn):
    return pl.BlockSpec((1, bn), lambda n, i: (0, n))


def _tile_spec(bn):
    return pl.BlockSpec((MM_BM, bn), lambda n, i: (i, n))


def _residual_epilogue(acc, extras, outs):
    b_ref, x_ref, gate_ref = extras
    outs[0][...] = x_ref[...] + gate_ref[...] * (acc + b_ref[...])


def _mm_residual(g, h, w, bias, x, mod, k_gate, name):
    m = x.shape[0]
    bn = MM_BN
    extras = [(bias.reshape(1, D), _row_spec(bn)), (x, _tile_spec(bn)), (mod, _mod_spec(g, k_gate, 0, bn))]
    return _mm(h, w, D, bn, _residual_epilogue, extras, jax.ShapeDtypeStruct((m, D), F32), _tile_spec(bn), name)


def _gelu_epilogue(acc, extras, outs):
    z = acc + extras[0][...]
    outs[0][...] = (0.5 * z * (1.0 + lax.erf(z * (2.0 ** -0.5)))).astype(BF16)


def _gmlp_gate_kernel(u_ref, v_ref, lg_ref, lb_ref, ws_ref, bs_ref, o_ref):
    v = v_ref[...].astype(F32)
    mu = jnp.mean(v, axis=-1, keepdims=True)
    vc = v - mu
    var = jnp.mean(vc * vc, axis=-1, keepdims=True)
    vn = (vc * lax.rsqrt(var + EPS) * lg_ref[...] + lb_ref[...]).astype(BF16)
    for grp in range(GMLP_GROUPS):
        cols = slice(grp * LANES, (grp + 1) * LANES)
        w = ws_ref[grp].astype(BF16)
        bias = bs_ref[:, grp : grp + 1]
        for c in range(TM // GMLP_CHUNK):
            rows = slice(c * GMLP_CHUNK, (c + 1) * GMLP_CHUNK)
            vm = jnp.dot(w, vn[rows, cols], preferred_element_type=F32) + bias
            o_ref[rows, cols] = (u_ref[rows, cols].astype(F32) * vm).astype(BF16)


def _gmlp_layer(g, x, mod, norm_g, w_in, b_in, ln_g, ln_b, w_s, b_s, w_out, b_out):
    m = x.shape[0]
    width = D
    h = _prenorm_call(g, x, norm_g, mod, 0)
    bn = MM_BN
    z = _mm(h, w_in, 2 * width, bn, _gelu_epilogue, [(b_in.reshape(1, 2 * width), _row_spec(bn))],
            jax.ShapeDtypeStruct((m, 2 * width), BF16), _tile_spec(bn), "gmlp_in")
    gated = pl.pallas_call(
        _gmlp_gate_kernel,
        grid=(m // TM,),
        in_specs=[
            pl.BlockSpec((TM, width), lambda i: (i, 0)),
            pl.BlockSpec((TM, width), lambda i: (i, 1)),
            pl.BlockSpec((1, width), lambda i: (0, 0)),
            pl.BlockSpec((1, width), lambda i: (0, 0)),
            pl.BlockSpec((GMLP_GROUPS, GMLP_CHUNK, GMLP_CHUNK), lambda i: (0, 0, 0)),
            pl.BlockSpec((GMLP_CHUNK, GMLP_GROUPS), lambda i: (0, 0)),
        ],
        out_specs=pl.BlockSpec((TM, width), lambda i: (i, 0)),
        out_shape=jax.ShapeDtypeStruct((m, width), BF16),
        compiler_params=_params(("arbitrary",)),
        name="gmlp_gate",
    )(z, z, ln_g.reshape(1, width), ln_b.reshape(1, width), w_s, b_s.T)
    return _mm_residual(g, gated, w_out, b_out, x, mod, 2, "gmlp_out")


def _rope_tables(g, d):
    nf = d // 4
    t = jnp.arange(g.lat_len)
    row = (t // GRID_W).astype(F32)
    col = (t % GRID_W).astype(F32)
    inv = ROPE_THETA ** (-jnp.arange(nf, dtype=F32) / nf)
    ang_r = row[:, None] * inv[None, :]
    ang_c = col[:, None] * inv[None, :]
    cos = jnp.concatenate([jnp.cos(ang_r)] * 2 + [jnp.cos(ang_c)] * 2, axis=-1)
    sin = jnp.concatenate([-jnp.sin(ang_r), jnp.sin(ang_r), -jnp.sin(ang_c), jnp.sin(ang_c)], axis=-1)
    reps = LANES // d
    cos = jnp.tile(cos, (1, reps))
    sin = jnp.tile(sin, (1, reps))
    cos = jnp.concatenate([jnp.ones((MM_BM, LANES), F32), cos], axis=0)
    sin = jnp.concatenate([jnp.zeros((MM_BM, LANES), F32), sin], axis=0)
    return cos, sin


def _rope_specs(g):
    return pl.BlockSpec((MM_BM, LANES), lambda n, i: (_rope_tile(g, i, MM_BM), 0))


def _rope128(x, cos, sin, nf):
    lane = lax.broadcasted_iota(I32, x.shape, 1)
    swapped = jnp.where((lane % (2 * nf)) < nf, pltpu.roll(x, LANES - nf, 1), pltpu.roll(x, nf, 1))
    return x * cos + swapped * sin


def _softmax_rows(s):
    s = s - jnp.max(s, axis=-1, keepdims=True)
    p = jnp.exp(s)
    return p / jnp.sum(p, axis=-1, keepdims=True)


def _dot_t(a, b):
    return lax.dot_general(a, b, (((1,), (1,)), ((), ())), preferred_element_type=F32)


def _rmsnorm_epilogue(acc, extras, outs):
    outs[0][...] = _rms(acc, extras[0][...]).astype(outs[0].dtype)


def _plain_epilogue(acc, extras, outs):
    outs[0][...] = acc.astype(outs[0].dtype)


def _pair_select(lane_lo, a, b):
    return jnp.where(lane_lo, a, b)


def _mla_q_epilogue(acc, extras, outs):
    gn_ref, gr_ref, cos_ref, sin_ref = extras
    qn_ref, qr_ref = outs
    nope_w = MLA_HEADS * MLA_NOPE
    lane_lo = lax.broadcasted_iota(I32, (acc.shape[0], LANES), 1) < MLA_ROPE
    cos, sin = cos_ref[...], sin_ref[...]
    for pair in range(MLA_HEADS // 2):
        r = acc[:, nope_w + pair * LANES : nope_w + (pair + 1) * LANES]
        r2 = r * r
        ss_lo = jnp.sum(jnp.where(lane_lo, r2, 0.0), axis=-1, keepdims=True)
        ss_hi = jnp.sum(jnp.where(lane_lo, 0.0, r2), axis=-1, keepdims=True)
        rinv = []
        for j, ss_r in enumerate((ss_lo, ss_hi)):
            h = 2 * pair + j
            qn = acc[:, h * LANES : (h + 1) * LANES]
            ri = lax.rsqrt((jnp.sum(qn * qn, axis=-1, keepdims=True) + ss_r) * (1.0 / MLA_QK) + EPS)
            qn_ref[:, h * LANES : (h + 1) * LANES] = (qn * ri * gn_ref[...]).astype(BF16)
            rinv.append(ri)
        rn = r * _pair_select(lane_lo, rinv[0], rinv[1]) * gr_ref[...]
        qr_ref[:, pair * LANES : (pair + 1) * LANES] = _rope128(rn, cos, sin, MLA_ROPE // 4).astype(BF16)


def _mla_kv_epilogue(acc, extras, outs):
    kr_in_ref, gn_ref, gr_ref, cos_ref, sin_ref = extras
    kn_ref, kr_ref, v_ref = outs
    nope_w = MLA_HEADS * MLA_NOPE
    lane_lo = lax.broadcasted_iota(I32, (acc.shape[0], LANES), 1) < MLA_ROPE
    kr = kr_in_ref[...]
    ss_r = jnp.sum(kr * kr, axis=-1, keepdims=True)
    kr2 = jnp.concatenate([kr, kr], axis=-1) * gr_ref[...]
    kr2 = _rope128(kr2, cos_ref[...], sin_ref[...], MLA_ROPE // 4)
    rinv = []
    for h in range(MLA_HEADS):
        kn = acc[:, h * LANES : (h + 1) * LANES]
        ri = lax.rsqrt((jnp.sum(kn * kn, axis=-1, keepdims=True) + ss_r) * (1.0 / MLA_QK) + EPS)
        kn_ref[:, h * LANES : (h + 1) * LANES] = (kn * ri * gn_ref[...]).astype(BF16)
        rinv.append(ri)
    for pair in range(MLA_HEADS // 2):
        scale = _pair_select(lane_lo, rinv[2 * pair], rinv[2 * pair + 1])
        kr_ref[:, pair * LANES : (pair + 1) * LANES] = (kr2 * scale).astype(BF16)
    v_ref[...] = acc[:, nope_w:].astype(BF16)


def _mla_attn_kernel(qn_ref, qr_ref, kn_ref, kr_ref, v_ref, o_ref):
    scale = MLA_QK ** -0.5
    for h in range(MLA_HEADS):
        cn = slice(h * MLA_NOPE, (h + 1) * MLA_NOPE)
        cr = slice(h * MLA_ROPE, (h + 1) * MLA_ROPE)
        s = _dot_t(qn_ref[:, cn], kn_ref[:, cn]) + _dot_t(qr_ref[:, cr], kr_ref[:, cr])
        p = _softmax_rows(s * scale)
        o_ref[:, cn] = jnp.dot(p.astype(BF16), v_ref[:, cn], preferred_element_type=F32).astype(BF16)


def _attention_call(kernel, q_arrays, kv_arrays, n_seq, q_len, kv_len, q_row0, out_width, extras, name):
    qb = q_len // TM
    q_specs = [pl.BlockSpec((TM, a.shape[1]), lambda s, j: (q_row0 // TM + s * qb + j, 0)) for a in q_arrays]
    kv_specs = [pl.BlockSpec((kv_len, a.shape[1]), lambda s, j: (s, 0)) for a in kv_arrays]
    return pl.pallas_call(
        kernel,
        grid=(n_seq, qb),
        in_specs=q_specs + kv_specs + [s for _, s in extras],
        out_specs=pl.BlockSpec((TM, out_width), lambda s, j: (s * qb + j, 0)),
        out_shape=jax.ShapeDtypeStruct((n_seq * q_len, out_width), BF16),
        compiler_params=_params(("arbitrary", "arbitrary")),
        name=name,
    )(*q_arrays, *kv_arrays, *[a for a, _ in extras])


def _latent_kv(g, own, cache):
    nc = _ctx_rows(g)
    own = own[nc:].reshape(g.n_lat, g.lat_len, own.shape[1])
    cache = cache.reshape(g.n_lat, g.past_len, cache.shape[1])
    return jnp.concatenate([cache, own], axis=1).reshape(g.n_lat * (g.past_len + g.lat_len), own.shape[2])


def _mla_layer(g, x, mod, norm_g, cache_ckv, cache_krope, w_dq, g_qa, w_uq, w_dkv, g_kva, w_kr, w_uk, w_uv,
               g_q, g_k, w_o):
    m = x.shape[0]
    nc, nl = _ctx_rows(g), _lat_rows(g)
    h = _prenorm_call(g, x, norm_g, mod, 0)
    cos, sin = _rope_tables(g, MLA_ROPE)
    rope_extras = [(cos, _rope_specs(g)), (sin, _rope_specs(g))]
    full = lambda w: pl.BlockSpec((1, w), lambda n, i: (0, 0))

    qa = _mm(h, w_dq, MLA_RANK, MLA_RANK, _rmsnorm_epilogue, [(g_qa.reshape(1, MLA_RANK), full(MLA_RANK))],
             jax.ShapeDtypeStruct((m, MLA_RANK), BF16), _tile_spec(MLA_RANK), "mla_dq")
    ckv = _mm(h, w_dkv, MLA_RANK, MLA_RANK, _rmsnorm_epilogue, [(g_kva.reshape(1, MLA_RANK), full(MLA_RANK))],
              jax.ShapeDtypeStruct((m, MLA_RANK), F32), _tile_spec(MLA_RANK), "mla_dkv")
    krope = _mm(h, w_kr, MLA_ROPE, MLA_ROPE, _plain_epilogue, [],
                jax.ShapeDtypeStruct((m, MLA_ROPE), F32), _tile_spec(MLA_ROPE), "mla_kr")

    w_uq3 = w_uq.reshape(MLA_RANK, MLA_HEADS, MLA_QK)
    w_uq_p = jnp.concatenate([w_uq3[:, :, :MLA_NOPE].reshape(MLA_RANK, -1),
                              w_uq3[:, :, MLA_NOPE:].reshape(MLA_RANK, -1)], axis=1)
    qw = w_uq_p.shape[1]
    gq_n = g_q[:MLA_NOPE].reshape(1, MLA_NOPE)
    gq_r = jnp.tile(g_q[MLA_NOPE:], 2).reshape(1, LANES)
    nope_w, rope_w = MLA_HEADS * MLA_NOPE, MLA_HEADS * MLA_ROPE
    qn, qr = _mm(qa, w_uq_p, qw, qw, _mla_q_epilogue,
                 [(gq_n, full(LANES)), (gq_r, full(LANES))] + rope_extras,
                 (jax.ShapeDtypeStruct((m, nope_w), BF16), jax.ShapeDtypeStruct((m, rope_w), BF16)),
                 (pl.BlockSpec((MM_BM, nope_w), lambda n, i: (i, 0)), pl.BlockSpec((MM_BM, rope_w), lambda n, i: (i, 0))),
                 "mla_uq")

    n_cache = g.n_lat * g.past_len
    ckv_all = jnp.concatenate([ckv, cache_ckv.reshape(n_cache, MLA_RANK)], axis=0)
    kr_all = jnp.concatenate([krope, cache_krope.reshape(n_cache, MLA_ROPE)], axis=0)
    w_ukv = jnp.concatenate([w_uk, w_uv], axis=1)
    gk_n = g_k[:MLA_NOPE].reshape(1, MLA_NOPE)
    gk_r = jnp.tile(g_k[MLA_NOPE:], 2).reshape(1, LANES)
    m_all = m + n_cache
    n_tok_tiles = m // MM_BM
    kv_rope = pl.BlockSpec((MM_BM, LANES), lambda n, i: (jnp.where(i < n_tok_tiles, _rope_tile(g, i, MM_BM), 0), 0))
    kn, kr, v = _mm(ckv_all, w_ukv, 2 * nope_w, 2 * nope_w, _mla_kv_epilogue,
                    [(kr_all, pl.BlockSpec((MM_BM, MLA_ROPE), lambda n, i: (i, 0))), (gk_n, full(LANES)),
                     (gk_r, full(LANES)), (cos, kv_rope), (sin, kv_rope)],
                    (jax.ShapeDtypeStruct((m_all, nope_w), BF16), jax.ShapeDtypeStruct((m_all, rope_w), BF16),
                     jax.ShapeDtypeStruct((m_all, nope_w), BF16)),
                    (pl.BlockSpec((MM_BM, nope_w), lambda n, i: (i, 0)), pl.BlockSpec((MM_BM, rope_w), lambda n, i: (i, 0)),
                     pl.BlockSpec((MM_BM, nope_w), lambda n, i: (i, 0))),
                    "mla_ukv")

    o_ctx = _attention_call(_mla_attn_kernel, [qn, qr], [kn, kr, v], g.n_ctx, g.ctx_len, g.ctx_len, 0,
                            nope_w, [], "mla_attn_ctx")
    lat_kv = [_latent_kv(g, a[:m], a[m:]) for a in (kn, kr, v)]
    o_lat = _attention_call(_mla_attn_kernel, [qn, qr], lat_kv, g.n_lat, g.lat_len, g.past_len + g.lat_len, nc,
                            nope_w, [], "mla_attn_lat")
    o = jnp.concatenate([o_ctx, o_lat], axis=0)
    x = _mm_residual(g, o, w_o, jnp.zeros((D,), F32), x, mod, 2, "mla_out")
    return x, ckv[:nc], krope[:nc]


def _diff_qk_epilogue(acc, extras, outs):
    g_ref, cos_ref, sin_ref = extras
    cos, sin = cos_ref[...], sin_ref[...]
    for j in range(acc.shape[1] // LANES):
        cols = slice(j * LANES, (j + 1) * LANES)
        y = _rms(acc[:, cols], g_ref[...])
        if len(outs) == 2:
            outs[1][:, cols] = y
        outs[0][:, cols] = _rope128(y, cos, sin, DIFF_DIM // 4).astype(BF16)


def _diff_v_epilogue(acc, extras, outs):
    outs[0][...] = acc.astype(BF16)
    outs[1][...] = acc


def _diff_attn_kernel(lam_init, q_ref, k_ref, v_ref, lam_ref, gs_ref, o_ref):
    lam = lam_ref[...]
    lam_full = (jnp.exp(jnp.sum(lam[0:1] * lam[1:2], axis=-1, keepdims=True))
                - jnp.exp(jnp.sum(lam[2:3] * lam[3:4], axis=-1, keepdims=True)) + lam_init)
    scale = DIFF_DIM ** -0.5
    for h in range(DIFF_HEADS):
        c0 = slice(2 * h * DIFF_DIM, (2 * h + 1) * DIFF_DIM)
        c1 = slice((2 * h + 1) * DIFF_DIM, (2 * h + 2) * DIFF_DIM)
        cv = slice(2 * h * DIFF_DIM, (2 * h + 2) * DIFF_DIM)
        p0 = _softmax_rows(_dot_t(q_ref[:, c0], k_ref[:, c0]) * scale)
        p1 = _softmax_rows(_dot_t(q_ref[:, c1], k_ref[:, c1]) * scale)
        p = p0 - lam_full * p1
        o = jnp.dot(p.astype(BF16), v_ref[:, cv], preferred_element_type=F32)
        o_ref[:, cv] = (_rms(o, gs_ref[...]) * (1.0 - lam_init)).astype(BF16)


def _diff_layer(g, x, mod, norm_g, cache_k, cache_v, w_qkv, g_q, g_k, lam, g_sub, w_o, lam_init):
    m = x.shape[0]
    nc = _ctx_rows(g)
    h = _prenorm_call(g, x, norm_g, mod, 0)
    cos, sin = _rope_tables(g, DIFF_DIM)
    rope_extras = [(cos, _rope_specs(g)), (sin, _rope_specs(g))]
    bn = MM_BN
    gain = lambda a: (a.reshape(1, DIFF_DIM), pl.BlockSpec((1, DIFF_DIM), lambda n, i: (0, 0)))
    (q,) = _mm(h, w_qkv, D, bn, _diff_qk_epilogue, [gain(g_q)] + rope_extras,
               (jax.ShapeDtypeStruct((m, D), BF16),), (_tile_spec(bn),), "diff_q")
    k, k_state = _mm(h, w_qkv, D, bn, _diff_qk_epilogue, [gain(g_k)] + rope_extras,
                     (jax.ShapeDtypeStruct((m, D), BF16), jax.ShapeDtypeStruct((m, D), F32)),
                     (_tile_spec(bn), _tile_spec(bn)), "diff_k", w_off=D // bn)
    v, v_state = _mm(h, w_qkv, D, bn, _diff_v_epilogue, [],
                     (jax.ShapeDtypeStruct((m, D), BF16), jax.ShapeDtypeStruct((m, D), F32)),
                     (_tile_spec(bn), _tile_spec(bn)), "diff_v", w_off=2 * D // bn)
    extras = [(lam, pl.BlockSpec((4, DIFF_DIM), lambda s, j: (0, 0))),
              (g_sub.reshape(1, 2 * DIFF_DIM), pl.BlockSpec((1, 2 * DIFF_DIM), lambda s, j: (0, 0)))]
    kern = functools.partial(_diff_attn_kernel, lam_init)
    o_ctx = _attention_call(kern, [q], [k, v], g.n_ctx, g.ctx_len, g.ctx_len, 0, D, extras, "diff_attn_ctx")
    n_cache = g.n_lat * g.past_len
    k_lat = _latent_kv(g, k, cache_k.reshape(n_cache, D).astype(BF16))
    v_lat = _latent_kv(g, v, cache_v.reshape(n_cache, D).astype(BF16))
    o_lat = _attention_call(kern, [q], [k_lat, v_lat], g.n_lat, g.lat_len, g.past_len + g.lat_len, nc, D,
                            extras, "diff_attn_lat")
    o = jnp.concatenate([o_ctx, o_lat], axis=0)
    x = _mm_residual(g, o, w_o, jnp.zeros((D,), F32), x, mod, 2, "diff_out")
    return x, k_state[:nc], v_state[:nc]


def _pool_kernel(h_ref, x_ref, gate_ref, a_ref, ic_ref, w_ref, b_ref, ps_ref, o_ref):
    hb = h_ref[...]
    win_sum = jnp.dot(a_ref[...], hb, preferred_element_type=F32)
    d = win_sum * ic_ref[...] - hb.astype(F32)
    y = jnp.dot(d.astype(BF16), w_ref[...].astype(BF16), preferred_element_type=F32) + b_ref[...]
    o_ref[...] = x_ref[...] + gate_ref[...] * (y * ps_ref[...])


def _pool_stream(h, x, n_seq, seq_len, row0, mod_row0, mod_rows_per_seq, mod, w, b, scale):
    t = jnp.arange(seq_len)
    bands, inv_counts = [], []
    for win in POOL_WINDOWS:
        lo = jnp.clip(t - win // 2, 0, seq_len)
        hi = jnp.clip(t + win // 2, 0, seq_len)
        bands.append(((t[None, :] >= lo[:, None]) & (t[None, :] < hi[:, None])).astype(BF16))
        inv_counts.append((1.0 / (hi - lo).astype(F32))[:, None])
    band = jnp.stack(bands)
    inv_count = jnp.stack(inv_counts)
    gd = POOL_GROUP_DIM
    seq0 = row0 // seq_len
    per = D // gd
    return pl.pallas_call(
        _pool_kernel,
        grid=(n_seq, len(POOL_WINDOWS)),
        in_specs=[
            pl.BlockSpec((seq_len, gd), lambda s, gi: (seq0 + s, gi)),
            pl.BlockSpec((seq_len, gd), lambda s, gi: (seq0 + s, gi)),
            pl.BlockSpec((None, 1, gd), lambda s, gi: (mod_row0 + s * mod_rows_per_seq, 0, 2 * per + gi)),
            pl.BlockSpec((None, seq_len, seq_len), lambda s, gi: (gi, 0, 0)),
            pl.BlockSpec((None, seq_len, 1), lambda s, gi: (gi, 0, 0)),
            pl.BlockSpec((None, gd, gd), lambda s, gi: (gi, 0, 0)),
            pl.BlockSpec((1, gd), lambda s, gi: (0, gi)),
            pl.BlockSpec((1, gd), lambda s, gi: (0, gi)),
        ],
        out_specs=pl.BlockSpec((seq_len, gd), lambda s, gi: (s, gi)),
        out_shape=jax.ShapeDtypeStruct((n_seq * seq_len, D), F32),
        compiler_params=_params(("arbitrary", "arbitrary")),
        name="pool",
    )(h, x, mod, band, inv_count, w, b.reshape(1, D), scale.reshape(1, D))


def _pool_layer(g, x, mod, norm_g, w, b, scale):
    nc = _ctx_rows(g)
    h = _prenorm_call(g, x, norm_g, mod, 0)
    xc = _pool_stream(h, x, g.n_ctx, g.ctx_len, 0, 0, 0, mod, w, b, scale)
    xl = _pool_stream(h, x, g.n_lat, g.lat_len, nc, 1, 1, mod, w, b, scale)
    return jnp.concatenate([xc, xl], axis=0)


def _split_bf16(a):
    hi = a.astype(BF16)
    return hi, (a - hi.astype(F32)).astype(BF16)


def _pack_bf16_pairs(h):
    bits = lax.bitcast_convert_type(h.astype(BF16).astype(F32), jnp.uint32)
    half = h.shape[1] // 2
    return (bits[:, :half] >> 16) | bits[:, half:]


def _unpack_bf16_pairs(w):
    lo = lax.bitcast_convert_type(w << 16, F32)
    hi = lax.bitcast_convert_type(w & jnp.uint32(0xFFFF0000), F32)
    return jnp.concatenate([lo, hi], axis=1).astype(BF16)


def _router_kernel(x_ref, g_ref, sh_ref, sc_ref, wr_ref, br_ref, h_ref, idx_ref, pos_ref, gate_ref, cnt_ref,
                   carry_ref):
    @pl.when(pl.program_id(0) == 0)
    def _():
        carry_ref[...] = jnp.zeros_like(carry_ref)

    h = _prenorm(x_ref[...], g_ref[...], sh_ref[...], sc_ref[...])
    h_ref[...] = _pack_bf16_pairs(h)
    h_hi, h_lo = _split_bf16(h)
    w_hi, w_lo = _split_bf16(wr_ref[...])
    dot = functools.partial(jnp.dot, preferred_element_type=F32)
    logits = dot(h_hi, w_hi) + dot(h_hi, w_lo) + dot(h_lo, w_hi) + br_ref[...]

    lane = lax.broadcasted_iota(I32, logits.shape, 1).astype(F32)
    vals, hots = [], []
    for k in range(TOP_K):
        top = jnp.max(logits, axis=-1, keepdims=True)
        sel = jnp.min(jnp.where(logits == top, lane, float(N_EXPERTS)), axis=-1, keepdims=True)
        hot = lane == sel
        idx_ref[:, k : k + 1] = sel.astype(I32)
        vals.append(top)
        hots.append(hot)
        logits = jnp.where(hot, -jnp.inf, logits)
    exps = [jnp.exp(v - vals[0]) for v in vals]
    denom = exps[0] + exps[1] + exps[2] + exps[3]
    for k in range(TOP_K):
        gate_ref[:, k : k + 1] = exps[k] / denom

    hot_all = sum(jnp.where(hot, 1.0, 0.0) for hot in hots)
    r = lax.broadcasted_iota(I32, (TM, TM), 0)
    c = lax.broadcasted_iota(I32, (TM, TM), 1)
    earlier = jnp.where(r > c, 1.0, 0.0).astype(BF16)
    base = dot(earlier, hot_all.astype(BF16)) + carry_ref[...]
    for k in range(TOP_K):
        pos_ref[:, k : k + 1] = jnp.sum(jnp.where(hots[k], base, 0.0), axis=-1, keepdims=True).astype(I32)
    carry_ref[...] = carry_ref[...] + jnp.sum(hot_all, axis=0, keepdims=True)
    cnt_ref[...] = carry_ref[...]


ROW_DMA_UNROLL = 8
PAIRS_PER_TILE = TM * TOP_K


def _dispatch_kernel(dest_ref, h_ref, xs_ref, sem):
    base = pl.program_id(0) * PAIRS_PER_TILE

    def issue(r, carry):
        for k in range(TOP_K):
            d = dest_ref[base + r * TOP_K + k]
            pltpu.make_async_copy(h_ref.at[pl.ds(r, 1), :], xs_ref.at[pl.ds(d, 1), :], sem).start()
        return carry

    lax.fori_loop(0, TM, issue, 0, unroll=ROW_DMA_UNROLL)
    rows = xs_ref.at[pl.ds(0, PAIRS_PER_TILE), :]
    pltpu.make_async_copy(rows, rows, sem).wait()


def _expert_mm_kernel(layer, col_tiles, compute, prio, nch, be_ref, br_ref, first_ref, nv_ref, na_ref, run_ref,
                      nxt_ref, nruns_ref, step_ref, plen_ref, x_ref, w_hbm, *rest):
    n_w = len(col_tiles)
    bias_refs, o_ref, (wbuf, wbf, sems) = rest[:n_w], rest[n_w], rest[n_w + 1 :]
    n, b = pl.program_id(0), pl.program_id(1)
    active = b < na_ref[0]
    kc = w_hbm.shape[2] // nch

    def w_copy(slot, nn, e, j, c=None):
        col = pl.multiple_of((col_tiles[j] + nn) * MOE_BN, MOE_BN)
        rows = slice(None) if c is None else pl.ds(c * kc, kc)
        return pltpu.make_async_copy(w_hbm.at[layer, e, rows, pl.ds(col, MOE_BN)], wbuf.at[slot, j, rows],
                                     sems.at[slot])

    def start_chunk(slot, nn, e, c):
        for j in range(n_w):
            w_copy(slot, nn, e, j, c).start(priority=prio)

    @pl.when(active)
    def _():
        visit = n * nruns_ref[0] + run_ref[b]
        slot = visit % 2

        @pl.when(first_ref[b] == 1)
        def _():
            issued = jnp.where(visit == 0, 0, jnp.minimum(plen_ref[b], nch))
            for c in range(nch):
                @pl.when(c >= issued)
                def _():
                    start_chunk(slot, n, be_ref[b], c)
            for j in range(n_w):
                w_copy(slot, n, be_ref[b], j).wait()

        nxt = nxt_ref[b]
        step = step_ref[b]
        for c in range(nch):
            @pl.when(jnp.logical_and(step == c, nxt >= 0))
            def _():
                start_chunk(1 - slot, n, nxt, c)

            @pl.when(jnp.logical_and(step == c, jnp.logical_and(nxt < 0, n + 1 < pl.num_programs(0))))
            def _():
                start_chunk(1 - slot, n + 1, be_ref[0], c)

        @pl.when(first_ref[b] == 1)
        def _():
            for j in range(n_w):
                wbf[j] = wbuf[slot, j].astype(BF16)

        compute(x_ref, wbf, bias_refs, o_ref, nv_ref[b])


def _expert_up_compute(x_ref, wbf, bias_refs, o_ref, n_valid):
    row = lax.broadcasted_iota(I32, (x_ref.shape[0], 1), 0)
    x = _unpack_bf16_pairs(jnp.where(row < n_valid, x_ref[...], jnp.uint32(0)))
    gate = jnp.dot(x, wbf[0], preferred_element_type=F32) + bias_refs[0][...]
    up = jnp.dot(x, wbf[1], preferred_element_type=F32) + bias_refs[1][...]
    gate = jnp.minimum(gate, SWIGLU_LIMIT)
    up = jnp.clip(up, -SWIGLU_LIMIT, SWIGLU_LIMIT)
    glu = gate * (1.0 / (1.0 + jnp.exp(-SWIGLU_ALPHA * gate)))
    o_ref[...] = ((up + 1.0) * glu).astype(BF16)


def _expert_down_compute(h_ref, wbf, bias_refs, o_ref, n_valid):
    o_ref[...] = jnp.dot(h_ref[...], wbf[0], preferred_element_type=F32) + bias_refs[0][...]


def _expert_mm(layer, sched, n_blocks, bm, prio, nch, x, w, bias4, col_tiles, n_col_tiles, compute, out_dtype, name):
    k = w.shape[2]
    n_w = len(col_tiles)
    n_rows = n_blocks * bm
    bias_spec = lambda off: pl.BlockSpec((None, None, 1, MOE_BN), lambda n, b, be, *_: (layer, be[b], 0, n + off))
    return pl.pallas_call(
        functools.partial(_expert_mm_kernel, layer, col_tiles, compute, prio, nch),
        grid_spec=pltpu.PrefetchScalarGridSpec(
            num_scalar_prefetch=len(sched),
            grid=(n_col_tiles, n_blocks),
            in_specs=[pl.BlockSpec((bm, x.shape[1]), lambda n, b, be, br, *_: (br[b], 0)),
                      pl.BlockSpec(memory_space=pl.ANY)] + [bias_spec(off) for off in col_tiles],
            out_specs=pl.BlockSpec((bm, MOE_BN), lambda n, b, be, br, *_: (br[b], n)),
            scratch_shapes=[pltpu.VMEM((2, n_w, k, MOE_BN), F32), pltpu.VMEM((n_w, k, MOE_BN), BF16),
                            pltpu.SemaphoreType.DMA((2,))],
        ),
        out_shape=jax.ShapeDtypeStruct((n_rows, n_col_tiles * MOE_BN), out_dtype),
        compiler_params=_params(("arbitrary", "arbitrary")),
        name=name,
    )(*sched, x, w, *([bias4] * n_w))


def _combine_kernel(dest_ref, ys_ref, gate_ref, x_ref, gmod_ref, o_ref, buf, sems):
    i = pl.program_id(0)

    def issue_tile(tile, slot):
        base = tile * PAIRS_PER_TILE

        def issue(r, carry):
            for k in range(TOP_K):
                d = dest_ref[base + r * TOP_K + k]
                pltpu.make_async_copy(ys_ref.at[pl.ds(d, 1), :], buf.at[slot, pl.ds(k * TM + r, 1), :],
                                      sems.at[slot]).start()
            return carry

        lax.fori_loop(0, TM, issue, 0, unroll=ROW_DMA_UNROLL)

    @pl.when(i == 0)
    def _():
        issue_tile(0, 0)

    slot = i % 2

    @pl.when(i + 1 < pl.num_programs(0))
    def _():
        issue_tile(i + 1, 1 - slot)

    pltpu.make_async_copy(ys_ref.at[pl.ds(0, PAIRS_PER_TILE), :], buf.at[slot], sems.at[slot]).wait()
    gates = gate_ref[...]
    y = gates[:, 0:1] * buf[slot, 0:TM, :]
    for k in range(1, TOP_K):
        y = y + gates[:, k : k + 1] * buf[slot, k * TM : (k + 1) * TM, :]
    o_ref[...] = x_ref[...] + gmod_ref[...] * y


def _moe_layer(g, layer, x, mod, norm_g, w_router, b_router, w_gate_up, b_gate_up, w_down, b_down):
    m = x.shape[0]
    n_tiles = m // TM
    depth = w_gate_up.shape[0]
    const = lambda shape: pl.BlockSpec(shape, lambda i: (0,) * len(shape))
    tile4 = pl.BlockSpec((TM, TOP_K), lambda i: (i, 0))
    h2, idx, pos, gates, counts = pl.pallas_call(
        _router_kernel,
        grid=(n_tiles,),
        in_specs=[pl.BlockSpec((TM, D), lambda i: (i, 0)), const((1, D)), _mod_spec(g, 3), _mod_spec(g, 4),
                  const((D, N_EXPERTS)), const((1, N_EXPERTS))],
        out_specs=(pl.BlockSpec((TM, D // 2), lambda i: (i, 0)), tile4, tile4, tile4, const((1, N_EXPERTS))),
        out_shape=(jax.ShapeDtypeStruct((m, D // 2), jnp.uint32), jax.ShapeDtypeStruct((m, TOP_K), I32),
                   jax.ShapeDtypeStruct((m, TOP_K), I32), jax.ShapeDtypeStruct((m, TOP_K), F32),
                   jax.ShapeDtypeStruct((1, N_EXPERTS), F32)),
        scratch_shapes=[pltpu.VMEM((1, N_EXPERTS), F32)],
        compiler_params=_params(("arbitrary",)),
        name="moe_router",
    )(x, norm_g.reshape(1, D), mod, mod, w_router, b_router.reshape(1, N_EXPERTS))

    bm, prio, nch = MOE_VARIANTS[layer % len(MOE_VARIANTS)]
    n_pairs = m * TOP_K
    n_blocks = -(-(n_pairs + N_EXPERTS * (bm - 1)) // bm)
    n_rows = n_blocks * bm
    counts = counts.reshape(N_EXPERTS).astype(I32)
    padded = (counts + bm - 1) // bm * bm
    pad_end = jnp.cumsum(padded)
    pad_start = pad_end - padded
    dest = (pad_start[idx] + pos).reshape(n_pairs)
    blk = jnp.arange(n_blocks, dtype=I32)
    n_active = (pad_end[-1] // bm).astype(I32)
    blk_row = jnp.minimum(blk, n_active - 1)
    blk_exp = jnp.minimum(jnp.sum(pad_end[None, :] <= (blk_row * bm)[:, None], axis=1), N_EXPERTS - 1).astype(I32)
    first = jnp.concatenate([jnp.ones((1,), I32), (blk_exp[1:] != blk_exp[:-1]).astype(I32)])
    n_valid = jnp.clip(counts[blk_exp] - (blk * bm - pad_start[blk_exp]), 0, bm).astype(I32)
    run_idx = jnp.cumsum(first) - 1
    n_runs = jnp.sum(first).astype(I32)
    used = jnp.concatenate([jnp.nonzero(counts > 0, size=N_EXPERTS, fill_value=-1)[0].astype(I32),
                            jnp.full((1,), -1, I32)])
    nxt_exp = used[run_idx + 1]
    step = (blk_row - pad_start[blk_exp] // bm).astype(I32)
    run_len = padded // bm
    prev_used = jnp.concatenate([used[n_runs - 1][None], used[:N_EXPERTS]])[run_idx]
    prev_len = run_len[prev_used].astype(I32)
    sched = (blk_exp, blk_row, first, n_valid, n_active.reshape(1), run_idx.astype(I32), nxt_exp, n_runs.reshape(1),
             step, prev_len)

    xs = pl.pallas_call(
        _dispatch_kernel,
        grid_spec=pltpu.PrefetchScalarGridSpec(
            num_scalar_prefetch=1,
            grid=(n_tiles,),
            in_specs=[pl.BlockSpec((TM, D // 2), lambda i, d: (i, 0))],
            out_specs=pl.BlockSpec(memory_space=pl.ANY),
            scratch_shapes=[pltpu.SemaphoreType.DMA],
        ),
        out_shape=jax.ShapeDtypeStruct((n_rows, D // 2), jnp.uint32),
        compiler_params=_params(("arbitrary",)),
        name="moe_dispatch",
    )(dest, h2)

    e_tiles = EXPERT_DIM // MOE_BN
    hid = _expert_mm(layer, sched, n_blocks, bm, prio, nch, xs, w_gate_up, b_gate_up.reshape(depth, N_EXPERTS, 1, 2 * EXPERT_DIM),
                     (0, e_tiles), e_tiles, _expert_up_compute, BF16, "moe_up")
    ys = _expert_mm(layer, sched, n_blocks, bm, prio, nch, hid, w_down, b_down.reshape(depth, N_EXPERTS, 1, D),
                    (0,), D // MOE_BN, _expert_down_compute, F32, "moe_down")

    return pl.pallas_call(
        _combine_kernel,
        grid_spec=pltpu.PrefetchScalarGridSpec(
            num_scalar_prefetch=1,
            grid=(n_tiles,),
            in_specs=[pl.BlockSpec(memory_space=pl.ANY),
                      pl.BlockSpec((TM, TOP_K), lambda i, d: (i, 0)),
                      pl.BlockSpec((TM, D), lambda i, d: (i, 0)),
                      pl.BlockSpec((None, 1, D), lambda i, d: (_mod_row(g, i), 0, 5))],
            out_specs=pl.BlockSpec((TM, D), lambda i, d: (i, 0)),
            scratch_shapes=[pltpu.VMEM((2, PAIRS_PER_TILE, D), F32), pltpu.SemaphoreType.DMA((2,))],
        ),
        out_shape=jax.ShapeDtypeStruct((m, D), F32),
        compiler_params=_params(("arbitrary",)),
        name="moe_combine",
    )(dest, ys, gates, x, mod)


def _trunk(g, x_prompt, x_sample, c, c_ctx, cache_mla_ckv, cache_mla_krope, cache_diff_k, cache_diff_v,
           norm1_g, norm2_g, w_mod, b_mod, moe_w_router, moe_b_router, moe_w_gate_up, moe_b_gate_up,
           moe_w_down, moe_b_down, gmlp, mla, diff, pool):
    depth = w_mod.shape[0]
    nc, nl = _ctx_rows(g), _lat_rows(g)
    x = jnp.concatenate([x_prompt.reshape(nc, D), x_sample.reshape(nl, D)], axis=0)
    cond8 = jnp.concatenate([c_ctx.reshape(1, D), c, jnp.zeros((8 - 1 - g.n_lat, D), F32)], axis=0)
    mods = _modulation(cond8, w_mod, b_mod)
    states = {}
    for l in range(depth):
        kind, j = l % 4, l // 4
        mod = mods[l]
        if kind == 0:
            x = _gmlp_layer(g, x, mod, norm1_g[l], *[p[j] for p in gmlp])
        elif kind == 1:
            x, ckv, krope = _mla_layer(g, x, mod, norm1_g[l], cache_mla_ckv[:, j], cache_mla_krope[:, j],
                                       *[p[j] for p in mla])
            states.setdefault("ckv", []).append(ckv.reshape(g.n_ctx, g.ctx_len, MLA_RANK))
            states.setdefault("krope", []).append(krope.reshape(g.n_ctx, g.ctx_len, MLA_ROPE))
        elif kind == 2:
            lam_init = 0.8 - 0.6 * math.exp(-0.3 * l)
            x, dk, dv = _diff_layer(g, x, mod, norm1_g[l], cache_diff_k[:, j], cache_diff_v[:, j],
                                    *[p[j] for p in diff], lam_init)
            shape = (g.n_ctx, g.ctx_len, DIFF_HEADS, 2 * DIFF_DIM)
            states.setdefault("dk", []).append(dk.reshape(shape))
            states.setdefault("dv", []).append(dv.reshape(shape))
        else:
            x = _pool_layer(g, x, mod, norm1_g[l], *[p[j] for p in pool])
        x = _moe_layer(g, l, x, mod, norm2_g[l], moe_w_router[l], moe_b_router[l], moe_w_gate_up, moe_b_gate_up,
                       moe_w_down, moe_b_down)
    y_prompt = x[:nc].reshape(x_prompt.shape)
    y_sample = x[nc:].reshape(x_sample.shape)
    return (y_prompt, y_sample, jnp.stack(states["ckv"], axis=1), jnp.stack(states["krope"], axis=1),
            jnp.stack(states["dk"], axis=1), jnp.stack(states["dv"], axis=1))


def kernel(x_prompt, x_sample, c, c_ctx, cache_mla_ckv, cache_mla_krope, cache_diff_k, cache_diff_v, norm1_g, norm2_g, w_mod, b_mod, moe_w_router, moe_b_router, moe_w_gate_up, moe_b_gate_up, moe_w_down, moe_b_down, gmlp_w_in, gmlp_b_in, gmlp_ln_g, gmlp_ln_b, gmlp_w_s, gmlp_b_s, gmlp_w_out, gmlp_b_out, mla_w_dq, mla_g_qa, mla_w_uq, mla_w_dkv, mla_g_kva, mla_w_kr, mla_w_uk, mla_w_uv, mla_g_q, mla_g_k, mla_w_o, diff_w_qkv, diff_g_q, diff_g_k, diff_lambda, diff_g_sub, diff_w_o, pool_w, pool_b, pool_scale):
    g = Geom(x_prompt.shape[0], x_prompt.shape[1], x_sample.shape[0], x_sample.shape[1], cache_mla_ckv.shape[2])
    gmlp = (gmlp_w_in, gmlp_b_in, gmlp_ln_g, gmlp_ln_b, gmlp_w_s, gmlp_b_s, gmlp_w_out, gmlp_b_out)
    mla = (mla_w_dq, mla_g_qa, mla_w_uq, mla_w_dkv, mla_g_kva, mla_w_kr, mla_w_uk, mla_w_uv, mla_g_q, mla_g_k,
           mla_w_o)
    diff = (diff_w_qkv, diff_g_q, diff_g_k, diff_lambda, diff_g_sub, diff_w_o)
    pool = (pool_w, pool_b, pool_scale)
    return _trunk(g, x_prompt, x_sample, c, c_ctx, cache_mla_ckv, cache_mla_krope, cache_diff_k, cache_diff_v,
                  norm1_g, norm2_g, w_mod, b_mod, moe_w_router, moe_b_router, moe_w_gate_up, moe_b_gate_up,
                  moe_w_down, moe_b_down, gmlp, mla, diff, pool)
```

```python
import collections
import functools
import math

import jax
import jax.numpy as jnp
from jax import lax
from jax.experimental import pallas as pl
from jax.experimental.pallas import tpu as pltpu

F32 = jnp.float32
BF16 = jnp.bfloat16
I32 = jnp.int32

D = 2048
EPS = 1e-6
ROPE_THETA = 10000.0
GRID_W = 64
TM = 256
MM_BM = 512
MM_BN = 1024
LANES = 128
VMEM_LIMIT = 56 * 1024 * 1024

GMLP_CHUNK = 128
GMLP_GROUPS = 16
MLA_HEADS = 16
MLA_RANK = 512
MLA_NOPE = 128
MLA_ROPE = 64
MLA_QK = MLA_NOPE + MLA_ROPE
DIFF_HEADS = 8
DIFF_DIM = 128
POOL_WINDOWS = (2, 4, 8, 16)
POOL_GROUP_DIM = 512
N_EXPERTS = 32
TOP_K = 4
EXPERT_DIM = 2048
SWIGLU_LIMIT = 7.0
SWIGLU_ALPHA = 1.702
MOE_BM = 256
MOE_UP_BN = 1024
MOE_DOWN_BN = 2048
WEIGHT_DMA_PRIORITY = 1

Geom = collections.namedtuple("Geom", "n_ctx ctx_len n_lat lat_len past_len")


def _ctx_rows(g):
    return g.n_ctx * g.ctx_len


def _lat_rows(g):
    return g.n_lat * g.lat_len


def _mod_row(g, i, bm=TM):
    ct = _ctx_rows(g) // bm
    return jnp.where(i < ct, 0, 1 + (i - ct) // (g.lat_len // bm))


def _rope_tile(g, i, bm):
    ct = _ctx_rows(g) // bm
    return jnp.where(i < ct, 0, 1 + (i - ct) % (g.lat_len // bm))


def _params(sem):
    return pltpu.CompilerParams(dimension_semantics=sem, vmem_limit_bytes=VMEM_LIMIT)


def _mod_spec(g, k, n_axis=None, bn=D):
    per = D // bn
    if n_axis is None:
        return pl.BlockSpec((None, 1, bn), lambda m: (_mod_row(g, m), 0, k * per))
    return pl.BlockSpec((None, 1, bn), lambda n, m: (_mod_row(g, m, MM_BM), 0, k * per + n))


def _rms(x, gain):
    return x * lax.rsqrt(jnp.mean(x * x, axis=-1, keepdims=True) + EPS) * gain


def _prenorm(x, gain, shift, scale):
    return _rms(x, gain) * (1.0 + scale) + shift


def _modulation_kernel(c_ref, w_ref, b_ref, o_ref):
    c = c_ref[...]
    s = c * (1.0 / (1.0 + jnp.exp(-c)))
    o_ref[...] = jnp.dot(s.astype(BF16), w_ref[...].astype(BF16), preferred_element_type=F32) + b_ref[...]


def _modulation(cond8, w_mod, b_mod):
    depth = w_mod.shape[0]
    bn = 1024
    out = pl.pallas_call(
        _modulation_kernel,
        grid=(depth, 6 * D // bn),
        in_specs=[
            pl.BlockSpec((8, D), lambda l, n: (0, 0)),
            pl.BlockSpec((None, D, bn), lambda l, n: (l, 0, n)),
            pl.BlockSpec((None, 1, bn), lambda l, n: (l, 0, n)),
        ],
        out_specs=pl.BlockSpec((None, 8, bn), lambda l, n: (l, 0, n)),
        out_shape=jax.ShapeDtypeStruct((depth, 8, 6 * D), F32),
        compiler_params=_params(("arbitrary", "arbitrary")),
        name="modulation",
    )(cond8, w_mod, b_mod.reshape(depth, 1, 6 * D))
    return out.reshape(depth, 8, 1, 6 * D)


def _prenorm_kernel(x_ref, g_ref, sh_ref, sc_ref, o_ref):
    o_ref[...] = _prenorm(x_ref[...], g_ref[...], sh_ref[...], sc_ref[...]).astype(o_ref.dtype)


def _prenorm_call(g, x, gain, mod, k_shift):
    m = x.shape[0]
    return pl.pallas_call(
        _prenorm_kernel,
        grid=(m // TM,),
        in_specs=[
            pl.BlockSpec((TM, D), lambda i: (i, 0)),
            pl.BlockSpec((1, D), lambda i: (0, 0)),
            _mod_spec(g, k_shift),
            _mod_spec(g, k_shift + 1),
        ],
        out_specs=pl.BlockSpec((TM, D), lambda i: (i, 0)),
        out_shape=jax.ShapeDtypeStruct((m, D), BF16),
        compiler_params=_params(("arbitrary",)),
        name="prenorm",
    )(x, gain.reshape(1, D), mod, mod)


def _mm_kernel(n_extra, epilogue, x_ref, w_ref, *rest):
    extras, outs, wbf_ref = rest[:n_extra], rest[n_extra:-1], rest[-1]

    @pl.when(pl.program_id(1) == 0)
    def _():
        wbf_ref[...] = w_ref[...].astype(BF16)

    acc = jnp.dot(x_ref[...].astype(BF16), wbf_ref[...], preferred_element_type=F32)
    epilogue(acc, extras, outs)


def _mm(x, w, ncols, bn, epilogue, extras, out_shape, out_specs, name, w_off=0):
    m, k = x.shape
    return pl.pallas_call(
        functools.partial(_mm_kernel, len(extras), epilogue),
        grid=(ncols // bn, m // MM_BM),
        in_specs=[
            pl.BlockSpec((MM_BM, k), lambda n, i: (i, 0)),
            pl.BlockSpec((k, bn), lambda n, i: (0, n + w_off)),
        ]
        + [s for _, s in extras],
        out_specs=out_specs,
        out_shape=out_shape,
        scratch_shapes=[pltpu.VMEM((k, bn), BF16)],
        compiler_params=_params(("arbitrary", "arbitrary")),
        name=name,
    )(x, w, *[a for a, _ in extras])


def _row_spec(bn):
    return pl.BlockSpec((1, bn), lambda n, i: (0, n))


def _tile_spec(bn):
    return pl.BlockSpec((MM_BM, bn), lambda n, i: (i, n))


def _residual_epilogue(acc, extras, outs):
    b_ref, x_ref, gate_ref = extras
    outs[0][...] = x_ref[...] + gate_ref[...] * (acc + b_ref[...])


def _mm_residual(g, h, w, bias, x, mod, k_gate, name):
    m = x.shape[0]
    bn = MM_BN
    extras = [(bias.reshape(1, D), _row_spec(bn)), (x, _tile_spec(bn)), (mod, _mod_spec(g, k_gate, 0, bn))]
    return _mm(h, w, D, bn, _residual_epilogue, extras, jax.ShapeDtypeStruct((m, D), F32), _tile_spec(bn), name)


def _gelu_epilogue(acc, extras, outs):
    z = acc + extras[0][...]
    outs[0][...] = (0.5 * z * (1.0 + lax.erf(z * (2.0 ** -0.5)))).astype(BF16)


def _gmlp_gate_kernel(u_ref, v_ref, lg_ref, lb_ref, ws_ref, bs_ref, o_ref):
    v = v_ref[...].astype(F32)
    mu = jnp.mean(v, axis=-1, keepdims=True)
    vc = v - mu
    var = jnp.mean(vc * vc, axis=-1, keepdims=True)
    vn = (vc * lax.rsqrt(var + EPS) * lg_ref[...] + lb_ref[...]).astype(BF16)
    for grp in range(GMLP_GROUPS):
        cols = slice(grp * LANES, (grp + 1) * LANES)
        w = ws_ref[grp].astype(BF16)
        bias = bs_ref[:, grp : grp + 1]
        for c in range(TM // GMLP_CHUNK):
            rows = slice(c * GMLP_CHUNK, (c + 1) * GMLP_CHUNK)
            vm = jnp.dot(w, vn[rows, cols], preferred_element_type=F32) + bias
            o_ref[rows, cols] = (u_ref[rows, cols].astype(F32) * vm).astype(BF16)


def _gmlp_layer(g, x, mod, norm_g, w_in, b_in, ln_g, ln_b, w_s, b_s, w_out, b_out):
    m = x.shape[0]
    width = D
    h = _prenorm_call(g, x, norm_g, mod, 0)
    bn = MM_BN
    z = _mm(h, w_in, 2 * width, bn, _gelu_epilogue, [(b_in.reshape(1, 2 * width), _row_spec(bn))],
            jax.ShapeDtypeStruct((m, 2 * width), BF16), _tile_spec(bn), "gmlp_in")
    gated = pl.pallas_call(
        _gmlp_gate_kernel,
        grid=(m // TM,),
        in_specs=[
            pl.BlockSpec((TM, width), lambda i: (i, 0)),
            pl.BlockSpec((TM, width), lambda i: (i, 1)),
            pl.BlockSpec((1, width), lambda i: (0, 0)),
            pl.BlockSpec((1, width), lambda i: (0, 0)),
            pl.BlockSpec((GMLP_GROUPS, GMLP_CHUNK, GMLP_CHUNK), lambda i: (0, 0, 0)),
            pl.BlockSpec((GMLP_CHUNK, GMLP_GROUPS), lambda i: (0, 0)),
        ],
        out_specs=pl.BlockSpec((TM, width), lambda i: (i, 0)),
        out_shape=jax.ShapeDtypeStruct((m, width), BF16),
        compiler_params=_params(("arbitrary",)),
        name="gmlp_gate",
    )(z, z, ln_g.reshape(1, width), ln_b.reshape(1, width), w_s, b_s.T)
    return _mm_residual(g, gated, w_out, b_out, x, mod, 2, "gmlp_out")


def _rope_tables(g, d):
    nf = d // 4
    t = jnp.arange(g.lat_len)
    row = (t // GRID_W).astype(F32)
    col = (t % GRID_W).astype(F32)
    inv = ROPE_THETA ** (-jnp.arange(nf, dtype=F32) / nf)
    ang_r = row[:, None] * inv[None, :]
    ang_c = col[:, None] * inv[None, :]
    cos = jnp.concatenate([jnp.cos(ang_r)] * 2 + [jnp.cos(ang_c)] * 2, axis=-1)
    sin = jnp.concatenate([-jnp.sin(ang_r), jnp.sin(ang_r), -jnp.sin(ang_c), jnp.sin(ang_c)], axis=-1)
    reps = LANES // d
    cos = jnp.tile(cos, (1, reps))
    sin = jnp.tile(sin, (1, reps))
    cos = jnp.concatenate([jnp.ones((MM_BM, LANES), F32), cos], axis=0)
    sin = jnp.concatenate([jnp.zeros((MM_BM, LANES), F32), sin], axis=0)
    return cos, sin


def _rope_specs(g):
    return pl.BlockSpec((MM_BM, LANES), lambda n, i: (_rope_tile(g, i, MM_BM), 0))


def _rope128(x, cos, sin, nf):
    lane = lax.broadcasted_iota(I32, x.shape, 1)
    swapped = jnp.where((lane % (2 * nf)) < nf, pltpu.roll(x, LANES - nf, 1), pltpu.roll(x, nf, 1))
    return x * cos + swapped * sin


def _softmax_rows(s):
    s = s - jnp.max(s, axis=-1, keepdims=True)
    p = jnp.exp(s)
    return p / jnp.sum(p, axis=-1, keepdims=True)


def _dot_t(a, b):
    return lax.dot_general(a, b, (((1,), (1,)), ((), ())), preferred_element_type=F32)


def _rmsnorm_epilogue(acc, extras, outs):
    outs[0][...] = _rms(acc, extras[0][...]).astype(outs[0].dtype)


def _plain_epilogue(acc, extras, outs):
    outs[0][...] = acc.astype(outs[0].dtype)


def _pair_select(lane_lo, a, b):
    return jnp.where(lane_lo, a, b)


def _mla_q_epilogue(acc, extras, outs):
    gn_ref, gr_ref, cos_ref, sin_ref = extras
    qn_ref, qr_ref = outs
    nope_w = MLA_HEADS * MLA_NOPE
    lane_lo = lax.broadcasted_iota(I32, (acc.shape[0], LANES), 1) < MLA_ROPE
    cos, sin = cos_ref[...], sin_ref[...]
    for pair in range(MLA_HEADS // 2):
        r = acc[:, nope_w + pair * LANES : nope_w + (pair + 1) * LANES]
        r2 = r * r
        ss_lo = jnp.sum(jnp.where(lane_lo, r2, 0.0), axis=-1, keepdims=True)
        ss_hi = jnp.sum(jnp.where(lane_lo, 0.0, r2), axis=-1, keepdims=True)
        rinv = []
        for j, ss_r in enumerate((ss_lo, ss_hi)):
            h = 2 * pair + j
            qn = acc[:, h * LANES : (h + 1) * LANES]
            ri = lax.rsqrt((jnp.sum(qn * qn, axis=-1, keepdims=True) + ss_r) * (1.0 / MLA_QK) + EPS)
            qn_ref[:, h * LANES : (h + 1) * LANES] = (qn * ri * gn_ref[...]).astype(BF16)
            rinv.append(ri)
        rn = r * _pair_select(lane_lo, rinv[0], rinv[1]) * gr_ref[...]
        qr_ref[:, pair * LANES : (pair + 1) * LANES] = _rope128(rn, cos, sin, MLA_ROPE // 4).astype(BF16)


def _mla_kv_epilogue(acc, extras, outs):
    kr_in_ref, gn_ref, gr_ref, cos_ref, sin_ref = extras
    kn_ref, kr_ref, v_ref = outs
    nope_w = MLA_HEADS * MLA_NOPE
    lane_lo = lax.broadcasted_iota(I32, (acc.shape[0], LANES), 1) < MLA_ROPE
    kr = kr_in_ref[...]
    ss_r = jnp.sum(kr * kr, axis=-1, keepdims=True)
    kr2 = jnp.concatenate([kr, kr], axis=-1) * gr_ref[...]
    kr2 = _rope128(kr2, cos_ref[...], sin_ref[...], MLA_ROPE // 4)
    rinv = []
    for h in range(MLA_HEADS):
        kn = acc[:, h * LANES : (h + 1) * LANES]
        ri = lax.rsqrt((jnp.sum(kn * kn, axis=-1, keepdims=True) + ss_r) * (1.0 / MLA_QK) + EPS)
        kn_ref[:, h * LANES : (h + 1) * LANES] = (kn * ri * gn_ref[...]).astype(BF16)
        rinv.append(ri)
    for pair in range(MLA_HEADS // 2):
        scale = _pair_select(lane_lo, rinv[2 * pair], rinv[2 * pair + 1])
        kr_ref[:, pair * LANES : (pair + 1) * LANES] = (kr2 * scale).astype(BF16)
    v_ref[...] = acc[:, nope_w:].astype(BF16)


def _mla_attn_kernel(qn_ref, qr_ref, kn_ref, kr_ref, v_ref, o_ref):
    scale = MLA_QK ** -0.5
    for h in range(MLA_HEADS):
        cn = slice(h * MLA_NOPE, (h + 1) * MLA_NOPE)
        cr = slice(h * MLA_ROPE, (h + 1) * MLA_ROPE)
        s = _dot_t(qn_ref[:, cn], kn_ref[:, cn]) + _dot_t(qr_ref[:, cr], kr_ref[:, cr])
        p = _softmax_rows(s * scale)
        o_ref[:, cn] = jnp.dot(p.astype(BF16), v_ref[:, cn], preferred_element_type=F32).astype(BF16)


def _attention_call(kernel, q_arrays, kv_arrays, n_seq, q_len, kv_len, q_row0, out_width, extras, name):
    qb = q_len // TM
    q_specs = [pl.BlockSpec((TM, a.shape[1]), lambda s, j: (q_row0 // TM + s * qb + j, 0)) for a in q_arrays]
    kv_specs = [pl.BlockSpec((kv_len, a.shape[1]), lambda s, j: (s, 0)) for a in kv_arrays]
    return pl.pallas_call(
        kernel,
        grid=(n_seq, qb),
        in_specs=q_specs + kv_specs + [s for _, s in extras],
        out_specs=pl.BlockSpec((TM, out_width), lambda s, j: (s * qb + j, 0)),
        out_shape=jax.ShapeDtypeStruct((n_seq * q_len, out_width), BF16),
        compiler_params=_params(("arbitrary", "arbitrary")),
        name=name,
    )(*q_arrays, *kv_arrays, *[a for a, _ in extras])


def _latent_kv(g, own, cache):
    nc = _ctx_rows(g)
    own = own[nc:].reshape(g.n_lat, g.lat_len, own.shape[1])
    cache = cache.reshape(g.n_lat, g.past_len, cache.shape[1])
    return jnp.concatenate([cache, own], axis=1).reshape(g.n_lat * (g.past_len + g.lat_len), own.shape[2])


def _mla_layer(g, x, mod, norm_g, cache_ckv, cache_krope, w_dq, g_qa, w_uq, w_dkv, g_kva, w_kr, w_uk, w_uv,
               g_q, g_k, w_o):
    m = x.shape[0]
    nc, nl = _ctx_rows(g), _lat_rows(g)
    h = _prenorm_call(g, x, norm_g, mod, 0)
    cos, sin = _rope_tables(g, MLA_ROPE)
    rope_extras = [(cos, _rope_specs(g)), (sin, _rope_specs(g))]
    full = lambda w: pl.BlockSpec((1, w), lambda n, i: (0, 0))

    qa = _mm(h, w_dq, MLA_RANK, MLA_RANK, _rmsnorm_epilogue, [(g_qa.reshape(1, MLA_RANK), full(MLA_RANK))],
             jax.ShapeDtypeStruct((m, MLA_RANK), BF16), _tile_spec(MLA_RANK), "mla_dq")
    ckv = _mm(h, w_dkv, MLA_RANK, MLA_RANK, _rmsnorm_epilogue, [(g_kva.reshape(1, MLA_RANK), full(MLA_RANK))],
              jax.ShapeDtypeStruct((m, MLA_RANK), F32), _tile_spec(MLA_RANK), "mla_dkv")
    krope = _mm(h, w_kr, MLA_ROPE, MLA_ROPE, _plain_epilogue, [],
                jax.ShapeDtypeStruct((m, MLA_ROPE), F32), _tile_spec(MLA_ROPE), "mla_kr")

    w_uq3 = w_uq.reshape(MLA_RANK, MLA_HEADS, MLA_QK)
    w_uq_p = jnp.concatenate([w_uq3[:, :, :MLA_NOPE].reshape(MLA_RANK, -1),
                              w_uq3[:, :, MLA_NOPE:].reshape(MLA_RANK, -1)], axis=1)
    qw = w_uq_p.shape[1]
    gq_n = g_q[:MLA_NOPE].reshape(1, MLA_NOPE)
    gq_r = jnp.tile(g_q[MLA_NOPE:], 2).reshape(1, LANES)
    nope_w, rope_w = MLA_HEADS * MLA_NOPE, MLA_HEADS * MLA_ROPE
    qn, qr = _mm(qa, w_uq_p, qw, qw, _mla_q_epilogue,
                 [(gq_n, full(LANES)), (gq_r, full(LANES))] + rope_extras,
                 (jax.ShapeDtypeStruct((m, nope_w), BF16), jax.ShapeDtypeStruct((m, rope_w), BF16)),
                 (pl.BlockSpec((MM_BM, nope_w), lambda n, i: (i, 0)), pl.BlockSpec((MM_BM, rope_w), lambda n, i: (i, 0))),
                 "mla_uq")

    n_cache = g.n_lat * g.past_len
    ckv_all = jnp.concatenate([ckv, cache_ckv.reshape(n_cache, MLA_RANK)], axis=0)
    kr_all = jnp.concatenate([krope, cache_krope.reshape(n_cache, MLA_ROPE)], axis=0)
    w_ukv = jnp.concatenate([w_uk, w_uv], axis=1)
    gk_n = g_k[:MLA_NOPE].reshape(1, MLA_NOPE)
    gk_r = jnp.tile(g_k[MLA_NOPE:], 2).reshape(1, LANES)
    m_all = m + n_cache
    n_tok_tiles = m // MM_BM
    kv_rope = pl.BlockSpec((MM_BM, LANES), lambda n, i: (jnp.where(i < n_tok_tiles, _rope_tile(g, i, MM_BM), 0), 0))
    kn, kr, v = _mm(ckv_all, w_ukv, 2 * nope_w, 2 * nope_w, _mla_kv_epilogue,
                    [(kr_all, pl.BlockSpec((MM_BM, MLA_ROPE), lambda n, i: (i, 0))), (gk_n, full(LANES)),
                     (gk_r, full(LANES)), (cos, kv_rope), (sin, kv_rope)],
                    (jax.ShapeDtypeStruct((m_all, nope_w), BF16), jax.ShapeDtypeStruct((m_all, rope_w), BF16),
                     jax.ShapeDtypeStruct((m_all, nope_w), BF16)),
                    (pl.BlockSpec((MM_BM, nope_w), lambda n, i: (i, 0)), pl.BlockSpec((MM_BM, rope_w), lambda n, i: (i, 0)),
                     pl.BlockSpec((MM_BM, nope_w), lambda n, i: (i, 0))),
                    "mla_ukv")

    o_ctx = _attention_call(_mla_attn_kernel, [qn, qr], [kn, kr, v], g.n_ctx, g.ctx_len, g.ctx_len, 0,
                            nope_w, [], "mla_attn_ctx")
    lat_kv = [_latent_kv(g, a[:m], a[m:]) for a in (kn, kr, v)]
    o_lat = _attention_call(_mla_attn_kernel, [qn, qr], lat_kv, g.n_lat, g.lat_len, g.past_len + g.lat_len, nc,
                            nope_w, [], "mla_attn_lat")
    o = jnp.concatenate([o_ctx, o_lat], axis=0)
    x = _mm_residual(g, o, w_o, jnp.zeros((D,), F32), x, mod, 2, "mla_out")
    return x, ckv[:nc], krope[:nc]


def _diff_qk_epilogue(acc, extras, outs):
    g_ref, cos_ref, sin_ref = extras
    cos, sin = cos_ref[...], sin_ref[...]
    for j in range(acc.shape[1] // LANES):
        cols = slice(j * LANES, (j + 1) * LANES)
        y = _rms(acc[:, cols], g_ref[...])
        if len(outs) == 2:
            outs[1][:, cols] = y
        outs[0][:, cols] = _rope128(y, cos, sin, DIFF_DIM // 4).astype(BF16)


def _diff_v_epilogue(acc, extras, outs):
    outs[0][...] = acc.astype(BF16)
    outs[1][...] = acc


def _diff_attn_kernel(lam_init, q_ref, k_ref, v_ref, lam_ref, gs_ref, o_ref):
    lam = lam_ref[...]
    lam_full = (jnp.exp(jnp.sum(lam[0:1] * lam[1:2], axis=-1, keepdims=True))
                - jnp.exp(jnp.sum(lam[2:3] * lam[3:4], axis=-1, keepdims=True)) + lam_init)
    scale = DIFF_DIM ** -0.5
    for h in range(DIFF_HEADS):
        c0 = slice(2 * h * DIFF_DIM, (2 * h + 1) * DIFF_DIM)
        c1 = slice((2 * h + 1) * DIFF_DIM, (2 * h + 2) * DIFF_DIM)
        cv = slice(2 * h * DIFF_DIM, (2 * h + 2) * DIFF_DIM)
        p0 = _softmax_rows(_dot_t(q_ref[:, c0], k_ref[:, c0]) * scale)
        p1 = _softmax_rows(_dot_t(q_ref[:, c1], k_ref[:, c1]) * scale)
        p = p0 - lam_full * p1
        o = jnp.dot(p.astype(BF16), v_ref[:, cv], preferred_element_type=F32)
        o_ref[:, cv] = (_rms(o, gs_ref[...]) * (1.0 - lam_init)).astype(BF16)


def _diff_layer(g, x, mod, norm_g, cache_k, cache_v, w_qkv, g_q, g_k, lam, g_sub, w_o, lam_init):
    m = x.shape[0]
    nc = _ctx_rows(g)
    h = _prenorm_call(g, x, norm_g, mod, 0)
    cos, sin = _rope_tables(g, DIFF_DIM)
    rope_extras = [(cos, _rope_specs(g)), (sin, _rope_specs(g))]
    bn = MM_BN
    gain = lambda a: (a.reshape(1, DIFF_DIM), pl.BlockSpec((1, DIFF_DIM), lambda n, i: (0, 0)))
    (q,) = _mm(h, w_qkv, D, bn, _diff_qk_epilogue, [gain(g_q)] + rope_extras,
               (jax.ShapeDtypeStruct((m, D), BF16),), (_tile_spec(bn),), "diff_q")
    k, k_state = _mm(h, w_qkv, D, bn, _diff_qk_epilogue, [gain(g_k)] + rope_extras,
                     (jax.ShapeDtypeStruct((m, D), BF16), jax.ShapeDtypeStruct((m, D), F32)),
                     (_tile_spec(bn), _tile_spec(bn)), "diff_k", w_off=D // bn)
    v, v_state = _mm(h, w_qkv, D, bn, _diff_v_epilogue, [],
                     (jax.ShapeDtypeStruct((m, D), BF16), jax.ShapeDtypeStruct((m, D), F32)),
                     (_tile_spec(bn), _tile_spec(bn)), "diff_v", w_off=2 * D // bn)
    extras = [(lam, pl.BlockSpec((4, DIFF_DIM), lambda s, j: (0, 0))),
              (g_sub.reshape(1, 2 * DIFF_DIM), pl.BlockSpec((1, 2 * DIFF_DIM), lambda s, j: (0, 0)))]
    kern = functools.partial(_diff_attn_kernel, lam_init)
    o_ctx = _attention_call(kern, [q], [k, v], g.n_ctx, g.ctx_len, g.ctx_len, 0, D, extras, "diff_attn_ctx")
    n_cache = g.n_lat * g.past_len
    k_lat = _latent_kv(g, k, cache_k.reshape(n_cache, D).astype(BF16))
    v_lat = _latent_kv(g, v, cache_v.reshape(n_cache, D).astype(BF16))
    o_lat = _attention_call(kern, [q], [k_lat, v_lat], g.n_lat, g.lat_len, g.past_len + g.lat_len, nc, D,
                            extras, "diff_attn_lat")
    o = jnp.concatenate([o_ctx, o_lat], axis=0)
    x = _mm_residual(g, o, w_o, jnp.zeros((D,), F32), x, mod, 2, "diff_out")
    return x, k_state[:nc], v_state[:nc]


def _pool_kernel(h_ref, x_ref, gate_ref, a_ref, ic_ref, w_ref, b_ref, ps_ref, o_ref):
    hb = h_ref[...]
    win_sum = jnp.dot(a_ref[...], hb, preferred_element_type=F32)
    d = win_sum * ic_ref[...] - hb.astype(F32)
    y = jnp.dot(d.astype(BF16), w_ref[...].astype(BF16), preferred_element_type=F32) + b_ref[...]
    o_ref[...] = x_ref[...] + gate_ref[...] * (y * ps_ref[...])


def _pool_stream(h, x, n_seq, seq_len, row0, mod_row0, mod_rows_per_seq, mod, w, b, scale):
    t = jnp.arange(seq_len)
    bands, inv_counts = [], []
    for win in POOL_WINDOWS:
        lo = jnp.clip(t - win // 2, 0, seq_len)
        hi = jnp.clip(t + win // 2, 0, seq_len)
        bands.append(((t[None, :] >= lo[:, None]) & (t[None, :] < hi[:, None])).astype(BF16))
        inv_counts.append((1.0 / (hi - lo).astype(F32))[:, None])
    band = jnp.stack(bands)
    inv_count = jnp.stack(inv_counts)
    gd = POOL_GROUP_DIM
    seq0 = row0 // seq_len
    per = D // gd
    return pl.pallas_call(
        _pool_kernel,
        grid=(n_seq, len(POOL_WINDOWS)),
        in_specs=[
            pl.BlockSpec((seq_len, gd), lambda s, gi: (seq0 + s, gi)),
            pl.BlockSpec((seq_len, gd), lambda s, gi: (seq0 + s, gi)),
            pl.BlockSpec((None, 1, gd), lambda s, gi: (mod_row0 + s * mod_rows_per_seq, 0, 2 * per + gi)),
            pl.BlockSpec((None, seq_len, seq_len), lambda s, gi: (gi, 0, 0)),
            pl.BlockSpec((None, seq_len, 1), lambda s, gi: (gi, 0, 0)),
            pl.BlockSpec((None, gd, gd), lambda s, gi: (gi, 0, 0)),
            pl.BlockSpec((1, gd), lambda s, gi: (0, gi)),
            pl.BlockSpec((1, gd), lambda s, gi: (0, gi)),
        ],
        out_specs=pl.BlockSpec((seq_len, gd), lambda s, gi: (s, gi)),
        out_shape=jax.ShapeDtypeStruct((n_seq * seq_len, D), F32),
        compiler_params=_params(("arbitrary", "arbitrary")),
        name="pool",
    )(h, x, mod, band, inv_count, w, b.reshape(1, D), scale.reshape(1, D))


def _pool_layer(g, x, mod, norm_g, w, b, scale):
    nc = _ctx_rows(g)
    h = _prenorm_call(g, x, norm_g, mod, 0)
    xc = _pool_stream(h, x, g.n_ctx, g.ctx_len, 0, 0, 0, mod, w, b, scale)
    xl = _pool_stream(h, x, g.n_lat, g.lat_len, nc, 1, 1, mod, w, b, scale)
    return jnp.concatenate([xc, xl], axis=0)


def _split_bf16(a):
    hi = a.astype(BF16)
    return hi, (a - hi.astype(F32)).astype(BF16)


def _pack_bf16_pairs(h):
    bits = lax.bitcast_convert_type(h.astype(BF16).astype(F32), jnp.uint32)
    half = h.shape[1] // 2
    return (bits[:, :half] >> 16) | bits[:, half:]


def _unpack_bf16_pairs(w):
    lo = lax.bitcast_convert_type(w << 16, F32)
    hi = lax.bitcast_convert_type(w & jnp.uint32(0xFFFF0000), F32)
    return jnp.concatenate([lo, hi], axis=1).astype(BF16)


def _router_kernel(x_ref, g_ref, sh_ref, sc_ref, wr_ref, br_ref, h_ref, idx_ref, pos_ref, gate_ref, cnt_ref,
                   carry_ref):
    @pl.when(pl.program_id(0) == 0)
    def _():
        carry_ref[...] = jnp.zeros_like(carry_ref)

    h = _prenorm(x_ref[...], g_ref[...], sh_ref[...], sc_ref[...])
    h_ref[...] = _pack_bf16_pairs(h)
    h_hi, h_lo = _split_bf16(h)
    w_hi, w_lo = _split_bf16(wr_ref[...])
    dot = functools.partial(jnp.dot, preferred_element_type=F32)
    logits = dot(h_hi, w_hi) + dot(h_hi, w_lo) + dot(h_lo, w_hi) + br_ref[...]

    lane = lax.broadcasted_iota(I32, logits.shape, 1).astype(F32)
    vals, hots = [], []
    for k in range(TOP_K):
        top = jnp.max(logits, axis=-1, keepdims=True)
        sel = jnp.min(jnp.where(logits == top, lane, float(N_EXPERTS)), axis=-1, keepdims=True)
        hot = lane == sel
        idx_ref[:, k : k + 1] = sel.astype(I32)
        vals.append(top)
        hots.append(hot)
        logits = jnp.where(hot, -jnp.inf, logits)
    exps = [jnp.exp(v - vals[0]) for v in vals]
    denom = exps[0] + exps[1] + exps[2] + exps[3]
    for k in range(TOP_K):
        gate_ref[:, k : k + 1] = exps[k] / denom

    hot_all = sum(jnp.where(hot, 1.0, 0.0) for hot in hots)
    r = lax.broadcasted_iota(I32, (TM, TM), 0)
    c = lax.broadcasted_iota(I32, (TM, TM), 1)
    earlier = jnp.where(r > c, 1.0, 0.0).astype(BF16)
    base = dot(earlier, hot_all.astype(BF16)) + carry_ref[...]
    for k in range(TOP_K):
        pos_ref[:, k : k + 1] = jnp.sum(jnp.where(hots[k], base, 0.0), axis=-1, keepdims=True).astype(I32)
    carry_ref[...] = carry_ref[...] + jnp.sum(hot_all, axis=0, keepdims=True)
    cnt_ref[...] = carry_ref[...]


ROW_DMA_UNROLL = 8
PAIRS_PER_TILE = TM * TOP_K


def _dispatch_kernel(dest_ref, h_ref, xs_ref, sem):
    base = pl.program_id(0) * PAIRS_PER_TILE

    def issue(r, carry):
        for k in range(TOP_K):
            d = dest_ref[base + r * TOP_K + k]
            pltpu.make_async_copy(h_ref.at[pl.ds(r, 1), :], xs_ref.at[pl.ds(d, 1), :], sem).start()
        return carry

    lax.fori_loop(0, TM, issue, 0, unroll=ROW_DMA_UNROLL)
    rows = xs_ref.at[pl.ds(0, PAIRS_PER_TILE), :]
    pltpu.make_async_copy(rows, rows, sem).wait()


def _expert_mm_kernel(layer, col_tiles, bn, compute, be_ref, br_ref, first_ref, nv_ref, na_ref, run_ref, nxt_ref,
                      nruns_ref, x_ref, w_hbm, *rest):
    n_w = len(col_tiles)
    bias_refs, o_ref, (wbuf, wbf, sems) = rest[:n_w], rest[n_w], rest[n_w + 1 :]
    n, b = pl.program_id(0), pl.program_id(1)
    active = b < na_ref[0]

    def w_copy(slot, nn, e, j):
        col = pl.multiple_of((col_tiles[j] + nn) * bn, bn)
        return pltpu.make_async_copy(w_hbm.at[layer, e, :, pl.ds(col, bn)], wbuf.at[slot, j], sems.at[slot])

    def prefetch(slot, nn, e):
        for j in range(n_w):
            w_copy(slot, nn, e, j).start(priority=WEIGHT_DMA_PRIORITY)

    @pl.when(jnp.logical_and(active, first_ref[b] == 1))
    def _():
        visit = n * nruns_ref[0] + run_ref[b]
        slot = visit % 2

        @pl.when(visit == 0)
        def _():
            prefetch(slot, n, be_ref[b])

        for j in range(n_w):
            w_copy(slot, n, be_ref[b], j).wait()
        nxt = nxt_ref[b]

        @pl.when(nxt >= 0)
        def _():
            prefetch(1 - slot, n, nxt)

        @pl.when(jnp.logical_and(nxt < 0, n + 1 < pl.num_programs(0)))
        def _():
            prefetch(1 - slot, n + 1, be_ref[0])

        for j in range(n_w):
            wbf[j] = wbuf[slot, j].astype(BF16)

    @pl.when(active)
    def _():
        compute(x_ref, wbf, bias_refs, o_ref, nv_ref[b])


def _expert_up_compute(x_ref, wbf, bias_refs, o_ref, n_valid):
    row = lax.broadcasted_iota(I32, (MOE_BM, 1), 0)
    x = _unpack_bf16_pairs(jnp.where(row < n_valid, x_ref[...], jnp.uint32(0)))
    gate = jnp.dot(x, wbf[0], preferred_element_type=F32) + bias_refs[0][...]
    up = jnp.dot(x, wbf[1], preferred_element_type=F32) + bias_refs[1][...]
    gate = jnp.minimum(gate, SWIGLU_LIMIT)
    up = jnp.clip(up, -SWIGLU_LIMIT, SWIGLU_LIMIT)
    glu = gate * (1.0 / (1.0 + jnp.exp(-SWIGLU_ALPHA * gate)))
    o_ref[...] = ((up + 1.0) * glu).astype(BF16)


def _expert_down_compute(h_ref, wbf, bias_refs, o_ref, n_valid):
    o_ref[...] = _pack_bf16_pairs(jnp.dot(h_ref[...], wbf[0], preferred_element_type=F32) + bias_refs[0][...])


def _expert_mm(layer, sched, n_blocks, x, w, bias4, col_tiles, n_col_tiles, bn, compute, out_bn, out_dtype, name):
    k = w.shape[2]
    n_w = len(col_tiles)
    n_rows = n_blocks * MOE_BM
    bias_spec = lambda off: pl.BlockSpec((None, None, 1, bn), lambda n, b, be, *_: (layer, be[b], 0, n + off))
    return pl.pallas_call(
        functools.partial(_expert_mm_kernel, layer, col_tiles, bn, compute),
        grid_spec=pltpu.PrefetchScalarGridSpec(
            num_scalar_prefetch=len(sched),
            grid=(n_col_tiles, n_blocks),
            in_specs=[pl.BlockSpec((MOE_BM, x.shape[1]), lambda n, b, be, br, *_: (br[b], 0)),
                      pl.BlockSpec(memory_space=pl.ANY)] + [bias_spec(off) for off in col_tiles],
            out_specs=pl.BlockSpec((MOE_BM, out_bn), lambda n, b, be, br, *_: (br[b], n)),
            scratch_shapes=[pltpu.VMEM((2, n_w, k, bn), F32), pltpu.VMEM((n_w, k, bn), BF16),
                            pltpu.SemaphoreType.DMA((2,))],
        ),
        out_shape=jax.ShapeDtypeStruct((n_rows, n_col_tiles * out_bn), out_dtype),
        compiler_params=_params(("arbitrary", "arbitrary")),
        name=name,
    )(*sched, x, w, *([bias4] * n_w))


def _combine_kernel(dest_ref, ys_ref, gate_ref, x_ref, gmod_ref, o_ref, buf, sems):
    i = pl.program_id(0)

    def issue_tile(tile, slot):
        base = tile * PAIRS_PER_TILE

        def issue(r, carry):
            for k in range(TOP_K):
                d = dest_ref[base + r * TOP_K + k]
                pltpu.make_async_copy(ys_ref.at[pl.ds(d, 1), :], buf.at[slot, pl.ds(k * TM + r, 1), :],
                                      sems.at[slot]).start()
            return carry

        lax.fori_loop(0, TM, issue, 0, unroll=ROW_DMA_UNROLL)

    @pl.when(i == 0)
    def _():
        issue_tile(0, 0)

    slot = i % 2

    @pl.when(i + 1 < pl.num_programs(0))
    def _():
        issue_tile(i + 1, 1 - slot)

    pltpu.make_async_copy(ys_ref.at[pl.ds(0, PAIRS_PER_TILE), :], buf.at[slot], sems.at[slot]).wait()
    gates = gate_ref[...]
    y = gates[:, 0:1] * _unpack_bf16_pairs(buf[slot, 0:TM, :]).astype(F32)
    for k in range(1, TOP_K):
        y = y + gates[:, k : k + 1] * _unpack_bf16_pairs(buf[slot, k * TM : (k + 1) * TM, :]).astype(F32)
    o_ref[...] = x_ref[...] + gmod_ref[...] * y


def _moe_layer(g, layer, x, mod, norm_g, w_router, b_router, w_gate_up, b_gate_up, w_down, b_down):
    m = x.shape[0]
    n_tiles = m // TM
    depth = w_gate_up.shape[0]
    const = lambda shape: pl.BlockSpec(shape, lambda i: (0,) * len(shape))
    tile4 = pl.BlockSpec((TM, TOP_K), lambda i: (i, 0))
    h2, idx, pos, gates, counts = pl.pallas_call(
        _router_kernel,
        grid=(n_tiles,),
        in_specs=[pl.BlockSpec((TM, D), lambda i: (i, 0)), const((1, D)), _mod_spec(g, 3), _mod_spec(g, 4),
                  const((D, N_EXPERTS)), const((1, N_EXPERTS))],
        out_specs=(pl.BlockSpec((TM, D // 2), lambda i: (i, 0)), tile4, tile4, tile4, const((1, N_EXPERTS))),
        out_shape=(jax.ShapeDtypeStruct((m, D // 2), jnp.uint32), jax.ShapeDtypeStruct((m, TOP_K), I32),
                   jax.ShapeDtypeStruct((m, TOP_K), I32), jax.ShapeDtypeStruct((m, TOP_K), F32),
                   jax.ShapeDtypeStruct((1, N_EXPERTS), F32)),
        scratch_shapes=[pltpu.VMEM((1, N_EXPERTS), F32)],
        compiler_params=_params(("arbitrary",)),
        name="moe_router",
    )(x, norm_g.reshape(1, D), mod, mod, w_router, b_router.reshape(1, N_EXPERTS))

    n_pairs = m * TOP_K
    n_blocks = -(-(n_pairs + N_EXPERTS * (MOE_BM - 1)) // MOE_BM)
    n_rows = n_blocks * MOE_BM
    counts = counts.reshape(N_EXPERTS).astype(I32)
    padded = (counts + MOE_BM - 1) // MOE_BM * MOE_BM
    pad_end = jnp.cumsum(padded)
    pad_start = pad_end - padded
    dest = (pad_start[idx] + pos).reshape(n_pairs)
    blk = jnp.arange(n_blocks, dtype=I32)
    n_active = (pad_end[-1] // MOE_BM).astype(I32)
    blk_row = jnp.minimum(blk, n_active - 1)
    blk_exp = jnp.minimum(jnp.sum(pad_end[None, :] <= (blk_row * MOE_BM)[:, None], axis=1), N_EXPERTS - 1).astype(I32)
    first = jnp.concatenate([jnp.ones((1,), I32), (blk_exp[1:] != blk_exp[:-1]).astype(I32)])
    n_valid = jnp.clip(counts[blk_exp] - (blk * MOE_BM - pad_start[blk_exp]), 0, MOE_BM).astype(I32)
    run_idx = jnp.cumsum(first) - 1
    n_runs = jnp.sum(first).astype(I32)
    has_rows = counts > 0
    rank = jnp.cumsum(has_rows) - 1
    slot_of = jnp.arange(N_EXPERTS + 1, dtype=I32)[:, None] == jnp.where(has_rows, rank, -1)[None, :]
    used = jnp.sum(jnp.where(slot_of, jnp.arange(N_EXPERTS, dtype=I32)[None, :] + 1, 0), axis=1).astype(I32) - 1
    nxt_exp = used[run_idx + 1]
    sched = (blk_exp, blk_row, first, n_valid, n_active.reshape(1), run_idx.astype(I32), nxt_exp, n_runs.reshape(1))

    xs = pl.pallas_call(
        _dispatch_kernel,
        grid_spec=pltpu.PrefetchScalarGridSpec(
            num_scalar_prefetch=1,
            grid=(n_tiles,),
            in_specs=[pl.BlockSpec((TM, D // 2), lambda i, d: (i, 0))],
            out_specs=pl.BlockSpec(memory_space=pl.ANY),
            scratch_shapes=[pltpu.SemaphoreType.DMA],
        ),
        out_shape=jax.ShapeDtypeStruct((n_rows, D // 2), jnp.uint32),
        compiler_params=_params(("arbitrary",)),
        name="moe_dispatch",
    )(dest, h2)

    e_tiles = EXPERT_DIM // MOE_UP_BN
    hid = _expert_mm(layer, sched, n_blocks, xs, w_gate_up, b_gate_up.reshape(depth, N_EXPERTS, 1, 2 * EXPERT_DIM),
                     (0, e_tiles), e_tiles, MOE_UP_BN, _expert_up_compute, MOE_UP_BN, BF16, "moe_up")
    ys = _expert_mm(layer, sched, n_blocks, hid, w_down, b_down.reshape(depth, N_EXPERTS, 1, D),
                    (0,), D // MOE_DOWN_BN, MOE_DOWN_BN, _expert_down_compute, MOE_DOWN_BN // 2, jnp.uint32, "moe_down")

    return pl.pallas_call(
        _combine_kernel,
        grid_spec=pltpu.PrefetchScalarGridSpec(
            num_scalar_prefetch=1,
            grid=(n_tiles,),
            in_specs=[pl.BlockSpec(memory_space=pl.ANY),
                      pl.BlockSpec((TM, TOP_K), lambda i, d: (i, 0)),
                      pl.BlockSpec((TM, D), lambda i, d: (i, 0)),
                      pl.BlockSpec((None, 1, D), lambda i, d: (_mod_row(g, i), 0, 5))],
            out_specs=pl.BlockSpec((TM, D), lambda i, d: (i, 0)),
            scratch_shapes=[pltpu.VMEM((2, PAIRS_PER_TILE, D // 2), jnp.uint32), pltpu.SemaphoreType.DMA((2,))],
        ),
        out_shape=jax.ShapeDtypeStruct((m, D), F32),
        compiler_params=_params(("arbitrary",)),
        name="moe_combine",
    )(dest, ys, gates, x, mod)


def _trunk(g, x_prompt, x_sample, c, c_ctx, cache_mla_ckv, cache_mla_krope, cache_diff_k, cache_diff_v,
           norm1_g, norm2_g, w_mod, b_mod, moe_w_router, moe_b_router, moe_w_gate_up, moe_b_gate_up,
           moe_w_down, moe_b_down, gmlp, mla, diff, pool):
    depth = w_mod.shape[0]
    nc, nl = _ctx_rows(g), _lat_rows(g)
    x = jnp.concatenate([x_prompt.reshape(nc, D), x_sample.reshape(nl, D)], axis=0)
    cond8 = jnp.concatenate([c_ctx.reshape(1, D), c, jnp.zeros((8 - 1 - g.n_lat, D), F32)], axis=0)
    mods = _modulation(cond8, w_mod, b_mod)
    states = {}
    for l in range(depth):
        kind, j = l % 4, l // 4
        mod = mods[l]
        if kind == 0:
            x = _gmlp_layer(g, x, mod, norm1_g[l], *[p[j] for p in gmlp])
        elif kind == 1:
            x, ckv, krope = _mla_layer(g, x, mod, norm1_g[l], cache_mla_ckv[:, j], cache_mla_krope[:, j],
                                       *[p[j] for p in mla])
            states.setdefault("ckv", []).append(ckv.reshape(g.n_ctx, g.ctx_len, MLA_RANK))
            states.setdefault("krope", []).append(krope.reshape(g.n_ctx, g.ctx_len, MLA_ROPE))
        elif kind == 2:
            lam_init = 0.8 - 0.6 * math.exp(-0.3 * l)
            x, dk, dv = _diff_layer(g, x, mod, norm1_g[l], cache_diff_k[:, j], cache_diff_v[:, j],
                                    *[p[j] for p in diff], lam_init)
            shape = (g.n_ctx, g.ctx_len, DIFF_HEADS, 2 * DIFF_DIM)
            states.setdefault("dk", []).append(dk.reshape(shape))
            states.setdefault("dv", []).append(dv.reshape(shape))
        else:
            x = _pool_layer(g, x, mod, norm1_g[l], *[p[j] for p in pool])
        x = _moe_layer(g, l, x, mod, norm2_g[l], moe_w_router[l], moe_b_router[l], moe_w_gate_up, moe_b_gate_up,
                       moe_w_down, moe_b_down)
    y_prompt = x[:nc].reshape(x_prompt.shape)
    y_sample = x[nc:].reshape(x_sample.shape)
    return (y_prompt, y_sample, jnp.stack(states["ckv"], axis=1), jnp.stack(states["krope"], axis=1),
            jnp.stack(states["dk"], axis=1), jnp.stack(states["dv"], axis=1))


def kernel(x_prompt, x_sample, c, c_ctx, cache_mla_ckv, cache_mla_krope, cache_diff_k, cache_diff_v, norm1_g, norm2_g, w_mod, b_mod, moe_w_router, moe_b_router, moe_w_gate_up, moe_b_gate_up, moe_w_down, moe_b_down, gmlp_w_in, gmlp_b_in, gmlp_ln_g, gmlp_ln_b, gmlp_w_s, gmlp_b_s, gmlp_w_out, gmlp_b_out, mla_w_dq, mla_g_qa, mla_w_uq, mla_w_dkv, mla_g_kva, mla_w_kr, mla_w_uk, mla_w_uv, mla_g_q, mla_g_k, mla_w_o, diff_w_qkv, diff_g_q, diff_g_k, diff_lambda, diff_g_sub, diff_w_o, pool_w, pool_b, pool_scale):
    g = Geom(x_prompt.shape[0], x_prompt.shape[1], x_sample.shape[0], x_sample.shape[1], cache_mla_ckv.shape[2])
    gmlp = (gmlp_w_in, gmlp_b_in, gmlp_ln_g, gmlp_ln_b, gmlp_w_s, gmlp_b_s, gmlp_w_out, gmlp_b_out)
    mla = (mla_w_dq, mla_g_qa, mla_w_uq, mla_w_dkv, mla_g_kva, mla_w_kr, mla_w_uk, mla_w_uv, mla_g_q, mla_g_k,
           mla_w_o)
    diff = (diff_w_qkv, diff_g_q, diff_g_k, diff_lambda, diff_g_sub, diff_w_o)
    pool = (pool_w, pool_b, pool_scale)
    return _trunk(g, x_prompt, x_sample, c, c_ctx, cache_mla_ckv, cache_mla_krope, cache_diff_k, cache_diff_v,
                  norm1_g, norm2_g, w_mod, b_mod, moe_w_router, moe_b_router, moe_w_gate_up, moe_b_gate_up,
                  moe_w_down, moe_b_down, gmlp, mla, diff, pool)
```

```python
import collections
import functools
import math

import jax
import jax.numpy as jnp
from jax import lax
from jax.experimental import pallas as pl
from jax.experimental.pallas import tpu as pltpu

F32 = jnp.float32
BF16 = jnp.bfloat16
I32 = jnp.int32

D = 2048
EPS = 1e-6
ROPE_THETA = 10000.0
GRID_W = 64
TM = 256
MM_BM = 512
MM_BN = 1024
LANES = 128
VMEM_LIMIT = 56 * 1024 * 1024

GMLP_CHUNK = 128
GMLP_GROUPS = 16
MLA_HEADS = 16
MLA_RANK = 512
MLA_NOPE = 128
MLA_ROPE = 64
MLA_QK = MLA_NOPE + MLA_ROPE
DIFF_HEADS = 8
DIFF_DIM = 128
POOL_WINDOWS = (2, 4, 8, 16)
POOL_GROUP_DIM = 512
N_EXPERTS = 32
TOP_K = 4
EXPERT_DIM = 2048
SWIGLU_LIMIT = 7.0
SWIGLU_ALPHA = 1.702
MOE_BM = 256
MOE_UP_BN = 1024
MOE_DOWN_BN = 2048
WEIGHT_DMA_PRIORITY = 1

Geom = collections.namedtuple("Geom", "n_ctx ctx_len n_lat lat_len past_len")


def _ctx_rows(g):
    return g.n_ctx * g.ctx_len


def _lat_rows(g):
    return g.n_lat * g.lat_len


def _mod_row(g, i, bm=TM):
    ct = _ctx_rows(g) // bm
    return jnp.where(i < ct, 0, 1 + (i - ct) // (g.lat_len // bm))


def _rope_tile(g, i, bm):
    ct = _ctx_rows(g) // bm
    return jnp.where(i < ct, 0, 1 + (i - ct) % (g.lat_len // bm))


def _params(sem):
    return pltpu.CompilerParams(dimension_semantics=sem, vmem_limit_bytes=VMEM_LIMIT)


def _mod_spec(g, k, n_axis=None, bn=D):
    per = D // bn
    if n_axis is None:
        return pl.BlockSpec((None, 1, bn), lambda m: (_mod_row(g, m), 0, k * per))
    return pl.BlockSpec((None, 1, bn), lambda n, m: (_mod_row(g, m, MM_BM), 0, k * per + n))


def _rms(x, gain):
    return x * lax.rsqrt(jnp.mean(x * x, axis=-1, keepdims=True) + EPS) * gain


def _prenorm(x, gain, shift, scale):
    return _rms(x, gain) * (1.0 + scale) + shift


def _modulation_kernel(c_ref, w_ref, b_ref, o_ref):
    c = c_ref[...]
    s = c * (1.0 / (1.0 + jnp.exp(-c)))
    o_ref[...] = jnp.dot(s.astype(BF16), w_ref[...].astype(BF16), preferred_element_type=F32) + b_ref[...]


def _modulation(cond8, w_mod, b_mod):
    depth = w_mod.shape[0]
    bn = 1024
    out = pl.pallas_call(
        _modulation_kernel,
        grid=(depth, 6 * D // bn),
        in_specs=[
            pl.BlockSpec((8, D), lambda l, n: (0, 0)),
            pl.BlockSpec((None, D, bn), lambda l, n: (l, 0, n)),
            pl.BlockSpec((None, 1, bn), lambda l, n: (l, 0, n)),
        ],
        out_specs=pl.BlockSpec((None, 8, bn), lambda l, n: (l, 0, n)),
        out_shape=jax.ShapeDtypeStruct((depth, 8, 6 * D), F32),
        compiler_params=_params(("arbitrary", "arbitrary")),
        name="modulation",
    )(cond8, w_mod, b_mod.reshape(depth, 1, 6 * D))
    return out.reshape(depth, 8, 1, 6 * D)


def _prenorm_kernel(x_ref, g_ref, sh_ref, sc_ref, o_ref):
    o_ref[...] = _prenorm(x_ref[...], g_ref[...], sh_ref[...], sc_ref[...]).astype(o_ref.dtype)


def _prenorm_call(g, x, gain, mod, k_shift):
    m = x.shape[0]
    return pl.pallas_call(
        _prenorm_kernel,
        grid=(m // TM,),
        in_specs=[
            pl.BlockSpec((TM, D), lambda i: (i, 0)),
            pl.BlockSpec((1, D), lambda i: (0, 0)),
            _mod_spec(g, k_shift),
            _mod_spec(g, k_shift + 1),
        ],
        out_specs=pl.BlockSpec((TM, D), lambda i: (i, 0)),
        out_shape=jax.ShapeDtypeStruct((m, D), BF16),
        compiler_params=_params(("arbitrary",)),
        name="prenorm",
    )(x, gain.reshape(1, D), mod, mod)


def _mm_kernel(n_extra, epilogue, x_ref, w_ref, *rest):
    extras, outs, wbf_ref = rest[:n_extra], rest[n_extra:-1], rest[-1]

    @pl.when(pl.program_id(1) == 0)
    def _():
        wbf_ref[...] = w_ref[...].astype(BF16)

    acc = jnp.dot(x_ref[...].astype(BF16), wbf_ref[...], preferred_element_type=F32)
    epilogue(acc, extras, outs)


def _mm(x, w, ncols, bn, epilogue, extras, out_shape, out_specs, name, w_off=0):
    m, k = x.shape
    return pl.pallas_call(
        functools.partial(_mm_kernel, len(extras), epilogue),
        grid=(ncols // bn, m // MM_BM),
        in_specs=[
            pl.BlockSpec((MM_BM, k), lambda n, i: (i, 0)),
            pl.BlockSpec((k, bn), lambda n, i: (0, n + w_off)),
        ]
        + [s for _, s in extras],
        out_specs=out_specs,
        out_shape=out_shape,
        scratch_shapes=[pltpu.VMEM((k, bn), BF16)],
        compiler_params=_params(("arbitrary", "arbitrary")),
        name=name,
    )(x, w, *[a for a, _ in extras])


def _row_spec(bn):
    return pl.BlockSpec((1, bn), lambda n, i: (0, n))


def _tile_spec(bn):
    return pl.BlockSpec((MM_BM, bn), lambda n, i: (i, n))


def _residual_epilogue(acc, extras, outs):
    b_ref, x_ref, gate_ref = extras
    outs[0][...] = x_ref[...] + gate_ref[...] * (acc + b_ref[...])


def _mm_residual(g, h, w, bias, x, mod, k_gate, name):
    m = x.shape[0]
    bn = MM_BN
    extras = [(bias.reshape(1, D), _row_spec(bn)), (x, _tile_spec(bn)), (mod, _mod_spec(g, k_gate, 0, bn))]
    return _mm(h, w, D, bn, _residual_epilogue, extras, jax.ShapeDtypeStruct((m, D), F32), _tile_spec(bn), name)


def _gelu_epilogue(acc, extras, outs):
    z = acc + extras[0][...]
    outs[0][...] = (0.5 * z * (1.0 + lax.erf(z * (2.0 ** -0.5)))).astype(BF16)


def _gmlp_gate_kernel(u_ref, v_ref, lg_ref, lb_ref, ws_ref, bs_ref, o_ref):
    v = v_ref[...].astype(F32)
    mu = jnp.mean(v, axis=-1, keepdims=True)
    vc = v - mu
    var = jnp.mean(vc * vc, axis=-1, keepdims=True)
    vn = (vc * lax.rsqrt(var + EPS) * lg_ref[...] + lb_ref[...]).astype(BF16)
    for grp in range(GMLP_GROUPS):
        cols = slice(grp * LANES, (grp + 1) * LANES)
        w = ws_ref[grp].astype(BF16)
        bias = bs_ref[:, grp : grp + 1]
        for c in range(TM // GMLP_CHUNK):
            rows = slice(c * GMLP_CHUNK, (c + 1) * GMLP_CHUNK)
            vm = jnp.dot(w, vn[rows, cols], preferred_element_type=F32) + bias
            o_ref[rows, cols] = (u_ref[rows, cols].astype(F32) * vm).astype(BF16)


def _gmlp_layer(g, x, mod, norm_g, w_in, b_in, ln_g, ln_b, w_s, b_s, w_out, b_out):
    m = x.shape[0]
    width = D
    h = _prenorm_call(g, x, norm_g, mod, 0)
    bn = MM_BN
    z = _mm(h, w_in, 2 * width, bn, _gelu_epilogue, [(b_in.reshape(1, 2 * width), _row_spec(bn))],
            jax.ShapeDtypeStruct((m, 2 * width), BF16), _tile_spec(bn), "gmlp_in")
    gated = pl.pallas_call(
        _gmlp_gate_kernel,
        grid=(m // TM,),
        in_specs=[
            pl.BlockSpec((TM, width), lambda i: (i, 0)),
            pl.BlockSpec((TM, width), lambda i: (i, 1)),
            pl.BlockSpec((1, width), lambda i: (0, 0)),
            pl.BlockSpec((1, width), lambda i: (0, 0)),
            pl.BlockSpec((GMLP_GROUPS, GMLP_CHUNK, GMLP_CHUNK), lambda i: (0, 0, 0)),
            pl.BlockSpec((GMLP_CHUNK, GMLP_GROUPS), lambda i: (0, 0)),
        ],
        out_specs=pl.BlockSpec((TM, width), lambda i: (i, 0)),
        out_shape=jax.ShapeDtypeStruct((m, width), BF16),
        compiler_params=_params(("arbitrary",)),
        name="gmlp_gate",
    )(z, z, ln_g.reshape(1, width), ln_b.reshape(1, width), w_s, b_s.T)
    return _mm_residual(g, gated, w_out, b_out, x, mod, 2, "gmlp_out")


def _rope_tables(g, d):
    nf = d // 4
    t = jnp.arange(g.lat_len)
    row = (t // GRID_W).astype(F32)
    col = (t % GRID_W).astype(F32)
    inv = ROPE_THETA ** (-jnp.arange(nf, dtype=F32) / nf)
    ang_r = row[:, None] * inv[None, :]
    ang_c = col[:, None] * inv[None, :]
    cos = jnp.concatenate([jnp.cos(ang_r)] * 2 + [jnp.cos(ang_c)] * 2, axis=-1)
    sin = jnp.concatenate([-jnp.sin(ang_r), jnp.sin(ang_r), -jnp.sin(ang_c), jnp.sin(ang_c)], axis=-1)
    reps = LANES // d
    cos = jnp.tile(cos, (1, reps))
    sin = jnp.tile(sin, (1, reps))
    cos = jnp.concatenate([jnp.ones((MM_BM, LANES), F32), cos], axis=0)
    sin = jnp.concatenate([jnp.zeros((MM_BM, LANES), F32), sin], axis=0)
    return cos, sin


def _rope_specs(g):
    return pl.BlockSpec((MM_BM, LANES), lambda n, i: (_rope_tile(g, i, MM_BM), 0))


def _rope128(x, cos, sin, nf):
    lane = lax.broadcasted_iota(I32, x.shape, 1)
    swapped = jnp.where((lane % (2 * nf)) < nf, pltpu.roll(x, LANES - nf, 1), pltpu.roll(x, nf, 1))
    return x * cos + swapped * sin


def _softmax_rows(s):
    s = s - jnp.max(s, axis=-1, keepdims=True)
    p = jnp.exp(s)
    return p / jnp.sum(p, axis=-1, keepdims=True)


def _dot_t(a, b):
    return lax.dot_general(a, b, (((1,), (1,)), ((), ())), preferred_element_type=F32)


def _rmsnorm_epilogue(acc, extras, outs):
    outs[0][...] = _rms(acc, extras[0][...]).astype(outs[0].dtype)


def _plain_epilogue(acc, extras, outs):
    outs[0][...] = acc.astype(outs[0].dtype)


def _pair_select(lane_lo, a, b):
    return jnp.where(lane_lo, a, b)


def _mla_q_epilogue(acc, extras, outs):
    gn_ref, gr_ref, cos_ref, sin_ref = extras
    qn_ref, qr_ref = outs
    nope_w = MLA_HEADS * MLA_NOPE
    lane_lo = lax.broadcasted_iota(I32, (acc.shape[0], LANES), 1) < MLA_ROPE
    cos, sin = cos_ref[...], sin_ref[...]
    for pair in range(MLA_HEADS // 2):
        r = acc[:, nope_w + pair * LANES : nope_w + (pair + 1) * LANES]
        r2 = r * r
        ss_lo = jnp.sum(jnp.where(lane_lo, r2, 0.0), axis=-1, keepdims=True)
        ss_hi = jnp.sum(jnp.where(lane_lo, 0.0, r2), axis=-1, keepdims=True)
        rinv = []
        for j, ss_r in enumerate((ss_lo, ss_hi)):
            h = 2 * pair + j
            qn = acc[:, h * LANES : (h + 1) * LANES]
            ri = lax.rsqrt((jnp.sum(qn * qn, axis=-1, keepdims=True) + ss_r) * (1.0 / MLA_QK) + EPS)
            qn_ref[:, h * LANES : (h + 1) * LANES] = (qn * ri * gn_ref[...]).astype(BF16)
            rinv.append(ri)
        rn = r * _pair_select(lane_lo, rinv[0], rinv[1]) * gr_ref[...]
        qr_ref[:, pair * LANES : (pair + 1) * LANES] = _rope128(rn, cos, sin, MLA_ROPE // 4).astype(BF16)


def _mla_kv_epilogue(acc, extras, outs):
    kr_in_ref, gn_ref, gr_ref, cos_ref, sin_ref = extras
    kn_ref, kr_ref, v_ref = outs
    nope_w = MLA_HEADS * MLA_NOPE
    lane_lo = lax.broadcasted_iota(I32, (acc.shape[0], LANES), 1) < MLA_ROPE
    kr = kr_in_ref[...]
    ss_r = jnp.sum(kr * kr, axis=-1, keepdims=True)
    kr2 = jnp.concatenate([kr, kr], axis=-1) * gr_ref[...]
    kr2 = _rope128(kr2, cos_ref[...], sin_ref[...], MLA_ROPE // 4)
    rinv = []
    for h in range(MLA_HEADS):
        kn = acc[:, h * LANES : (h + 1) * LANES]
        ri = lax.rsqrt((jnp.sum(kn * kn, axis=-1, keepdims=True) + ss_r) * (1.0 / MLA_QK) + EPS)
        kn_ref[:, h * LANES : (h + 1) * LANES] = (kn * ri * gn_ref[...]).astype(BF16)
        rinv.append(ri)
    for pair in range(MLA_HEADS // 2):
        scale = _pair_select(lane_lo, rinv[2 * pair], rinv[2 * pair + 1])
        kr_ref[:, pair * LANES : (pair + 1) * LANES] = (kr2 * scale).astype(BF16)
    v_ref[...] = acc[:, nope_w:].astype(BF16)


def _mla_attn_kernel(qn_ref, qr_ref, kn_ref, kr_ref, v_ref, o_ref):
    scale = MLA_QK ** -0.5
    for h in range(MLA_HEADS):
        cn = slice(h * MLA_NOPE, (h + 1) * MLA_NOPE)
        cr = slice(h * MLA_ROPE, (h + 1) * MLA_ROPE)
        s = _dot_t(qn_ref[:, cn], kn_ref[:, cn]) + _dot_t(qr_ref[:, cr], kr_ref[:, cr])
        p = _softmax_rows(s * scale)
        o_ref[:, cn] = jnp.dot(p.astype(BF16), v_ref[:, cn], preferred_element_type=F32).astype(BF16)


def _attention_call(kernel, q_arrays, kv_arrays, n_seq, q_len, kv_len, q_row0, out_width, extras, name):
    qb = q_len // TM
    q_specs = [pl.BlockSpec((TM, a.shape[1]), lambda s, j: (q_row0 // TM + s * qb + j, 0)) for a in q_arrays]
    kv_specs = [pl.BlockSpec((kv_len, a.shape[1]), lambda s, j: (s, 0)) for a in kv_arrays]
    return pl.pallas_call(
        kernel,
        grid=(n_seq, qb),
        in_specs=q_specs + kv_specs + [s for _, s in extras],
        out_specs=pl.BlockSpec((TM, out_width), lambda s, j: (s * qb + j, 0)),
        out_shape=jax.ShapeDtypeStruct((n_seq * q_len, out_width), BF16),
        compiler_params=_params(("arbitrary", "arbitrary")),
        name=name,
    )(*q_arrays, *kv_arrays, *[a for a, _ in extras])


def _latent_kv(g, own, cache):
    nc = _ctx_rows(g)
    own = own[nc:].reshape(g.n_lat, g.lat_len, own.shape[1])
    cache = cache.reshape(g.n_lat, g.past_len, cache.shape[1])
    return jnp.concatenate([cache, own], axis=1).reshape(g.n_lat * (g.past_len + g.lat_len), own.shape[2])


def _mla_layer(g, x, mod, norm_g, cache_ckv, cache_krope, w_dq, g_qa, w_uq, w_dkv, g_kva, w_kr, w_uk, w_uv,
               g_q, g_k, w_o):
    m = x.shape[0]
    nc, nl = _ctx_rows(g), _lat_rows(g)
    h = _prenorm_call(g, x, norm_g, mod, 0)
    cos, sin = _rope_tables(g, MLA_ROPE)
    rope_extras = [(cos, _rope_specs(g)), (sin, _rope_specs(g))]
    full = lambda w: pl.BlockSpec((1, w), lambda n, i: (0, 0))

    qa = _mm(h, w_dq, MLA_RANK, MLA_RANK, _rmsnorm_epilogue, [(g_qa.reshape(1, MLA_RANK), full(MLA_RANK))],
             jax.ShapeDtypeStruct((m, MLA_RANK), BF16), _tile_spec(MLA_RANK), "mla_dq")
    ckv = _mm(h, w_dkv, MLA_RANK, MLA_RANK, _rmsnorm_epilogue, [(g_kva.reshape(1, MLA_RANK), full(MLA_RANK))],
              jax.ShapeDtypeStruct((m, MLA_RANK), F32), _tile_spec(MLA_RANK), "mla_dkv")
    krope = _mm(h, w_kr, MLA_ROPE, MLA_ROPE, _plain_epilogue, [],
                jax.ShapeDtypeStruct((m, MLA_ROPE), F32), _tile_spec(MLA_ROPE), "mla_kr")

    w_uq3 = w_uq.reshape(MLA_RANK, MLA_HEADS, MLA_QK)
    w_uq_p = jnp.concatenate([w_uq3[:, :, :MLA_NOPE].reshape(MLA_RANK, -1),
                              w_uq3[:, :, MLA_NOPE:].reshape(MLA_RANK, -1)], axis=1)
    qw = w_uq_p.shape[1]
    gq_n = g_q[:MLA_NOPE].reshape(1, MLA_NOPE)
    gq_r = jnp.tile(g_q[MLA_NOPE:], 2).reshape(1, LANES)
    nope_w, rope_w = MLA_HEADS * MLA_NOPE, MLA_HEADS * MLA_ROPE
    qn, qr = _mm(qa, w_uq_p, qw, qw, _mla_q_epilogue,
                 [(gq_n, full(LANES)), (gq_r, full(LANES))] + rope_extras,
                 (jax.ShapeDtypeStruct((m, nope_w), BF16), jax.ShapeDtypeStruct((m, rope_w), BF16)),
                 (pl.BlockSpec((MM_BM, nope_w), lambda n, i: (i, 0)), pl.BlockSpec((MM_BM, rope_w), lambda n, i: (i, 0))),
                 "mla_uq")

    n_cache = g.n_lat * g.past_len
    ckv_all = jnp.concatenate([ckv, cache_ckv.reshape(n_cache, MLA_RANK)], axis=0)
    kr_all = jnp.concatenate([krope, cache_krope.reshape(n_cache, MLA_ROPE)], axis=0)
    w_ukv = jnp.concatenate([w_uk, w_uv], axis=1)
    gk_n = g_k[:MLA_NOPE].reshape(1, MLA_NOPE)
    gk_r = jnp.tile(g_k[MLA_NOPE:], 2).reshape(1, LANES)
    m_all = m + n_cache
    n_tok_tiles = m // MM_BM
    kv_rope = pl.BlockSpec((MM_BM, LANES), lambda n, i: (jnp.where(i < n_tok_tiles, _rope_tile(g, i, MM_BM), 0), 0))
    kn, kr, v = _mm(ckv_all, w_ukv, 2 * nope_w, 2 * nope_w, _mla_kv_epilogue,
                    [(kr_all, pl.BlockSpec((MM_BM, MLA_ROPE), lambda n, i: (i, 0))), (gk_n, full(LANES)),
                     (gk_r, full(LANES)), (cos, kv_rope), (sin, kv_rope)],
                    (jax.ShapeDtypeStruct((m_all, nope_w), BF16), jax.ShapeDtypeStruct((m_all, rope_w), BF16),
                     jax.ShapeDtypeStruct((m_all, nope_w), BF16)),
                    (pl.BlockSpec((MM_BM, nope_w), lambda n, i: (i, 0)), pl.BlockSpec((MM_BM, rope_w), lambda n, i: (i, 0)),
                     pl.BlockSpec((MM_BM, nope_w), lambda n, i: (i, 0))),
                    "mla_ukv")

    o_ctx = _attention_call(_mla_attn_kernel, [qn, qr], [kn, kr, v], g.n_ctx, g.ctx_len, g.ctx_len, 0,
                            nope_w, [], "mla_attn_ctx")
    lat_kv = [_latent_kv(g, a[:m], a[m:]) for a in (kn, kr, v)]
    o_lat = _attention_call(_mla_attn_kernel, [qn, qr], lat_kv, g.n_lat, g.lat_len, g.past_len + g.lat_len, nc,
                            nope_w, [], "mla_attn_lat")
    o = jnp.concatenate([o_ctx, o_lat], axis=0)
    x = _mm_residual(g, o, w_o, jnp.zeros((D,), F32), x, mod, 2, "mla_out")
    return x, ckv[:nc], krope[:nc]


def _diff_qk_epilogue(acc, extras, outs):
    g_ref, cos_ref, sin_ref = extras
    cos, sin = cos_ref[...], sin_ref[...]
    for j in range(acc.shape[1] // LANES):
        cols = slice(j * LANES, (j + 1) * LANES)
        y = _rms(acc[:, cols], g_ref[...])
        if len(outs) == 2:
            outs[1][:, cols] = y
        outs[0][:, cols] = _rope128(y, cos, sin, DIFF_DIM // 4).astype(BF16)


def _diff_v_epilogue(acc, extras, outs):
    outs[0][...] = acc.astype(BF16)
    outs[1][...] = acc


def _diff_attn_kernel(lam_init, q_ref, k_ref, v_ref, lam_ref, gs_ref, o_ref):
    lam = lam_ref[...]
    lam_full = (jnp.exp(jnp.sum(lam[0:1] * lam[1:2], axis=-1, keepdims=True))
                - jnp.exp(jnp.sum(lam[2:3] * lam[3:4], axis=-1, keepdims=True)) + lam_init)
    scale = DIFF_DIM ** -0.5
    for h in range(DIFF_HEADS):
        c0 = slice(2 * h * DIFF_DIM, (2 * h + 1) * DIFF_DIM)
        c1 = slice((2 * h + 1) * DIFF_DIM, (2 * h + 2) * DIFF_DIM)
        cv = slice(2 * h * DIFF_DIM, (2 * h + 2) * DIFF_DIM)
        p0 = _softmax_rows(_dot_t(q_ref[:, c0], k_ref[:, c0]) * scale)
        p1 = _softmax_rows(_dot_t(q_ref[:, c1], k_ref[:, c1]) * scale)
        p = p0 - lam_full * p1
        o = jnp.dot(p.astype(BF16), v_ref[:, cv], preferred_element_type=F32)
        o_ref[:, cv] = (_rms(o, gs_ref[...]) * (1.0 - lam_init)).astype(BF16)


def _diff_layer(g, x, mod, norm_g, cache_k, cache_v, w_qkv, g_q, g_k, lam, g_sub, w_o, lam_init):
    m = x.shape[0]
    nc = _ctx_rows(g)
    h = _prenorm_call(g, x, norm_g, mod, 0)
    cos, sin = _rope_tables(g, DIFF_DIM)
    rope_extras = [(cos, _rope_specs(g)), (sin, _rope_specs(g))]
    bn = MM_BN
    gain = lambda a: (a.reshape(1, DIFF_DIM), pl.BlockSpec((1, DIFF_DIM), lambda n, i: (0, 0)))
    (q,) = _mm(h, w_qkv, D, bn, _diff_qk_epilogue, [gain(g_q)] + rope_extras,
               (jax.ShapeDtypeStruct((m, D), BF16),), (_tile_spec(bn),), "diff_q")
    k, k_state = _mm(h, w_qkv, D, bn, _diff_qk_epilogue, [gain(g_k)] + rope_extras,
                     (jax.ShapeDtypeStruct((m, D), BF16), jax.ShapeDtypeStruct((m, D), F32)),
                     (_tile_spec(bn), _tile_spec(bn)), "diff_k", w_off=D // bn)
    v, v_state = _mm(h, w_qkv, D, bn, _diff_v_epilogue, [],
                     (jax.ShapeDtypeStruct((m, D), BF16), jax.ShapeDtypeStruct((m, D), F32)),
                     (_tile_spec(bn), _tile_spec(bn)), "diff_v", w_off=2 * D // bn)
    extras = [(lam, pl.BlockSpec((4, DIFF_DIM), lambda s, j: (0, 0))),
              (g_sub.reshape(1, 2 * DIFF_DIM), pl.BlockSpec((1, 2 * DIFF_DIM), lambda s, j: (0, 0)))]
    kern = functools.partial(_diff_attn_kernel, lam_init)
    o_ctx = _attention_call(kern, [q], [k, v], g.n_ctx, g.ctx_len, g.ctx_len, 0, D, extras, "diff_attn_ctx")
    n_cache = g.n_lat * g.past_len
    k_lat = _latent_kv(g, k, cache_k.reshape(n_cache, D).astype(BF16))
    v_lat = _latent_kv(g, v, cache_v.reshape(n_cache, D).astype(BF16))
    o_lat = _attention_call(kern, [q], [k_lat, v_lat], g.n_lat, g.lat_len, g.past_len + g.lat_len, nc, D,
                            extras, "diff_attn_lat")
    o = jnp.concatenate([o_ctx, o_lat], axis=0)
    x = _mm_residual(g, o, w_o, jnp.zeros((D,), F32), x, mod, 2, "diff_out")
    return x, k_state[:nc], v_state[:nc]


def _pool_kernel(h_ref, x_ref, gate_ref, a_ref, ic_ref, w_ref, b_ref, ps_ref, o_ref):
    hb = h_ref[...]
    win_sum = jnp.dot(a_ref[...], hb, preferred_element_type=F32)
    d = win_sum * ic_ref[...] - hb.astype(F32)
    y = jnp.dot(d.astype(BF16), w_ref[...].astype(BF16), preferred_element_type=F32) + b_ref[...]
    o_ref[...] = x_ref[...] + gate_ref[...] * (y * ps_ref[...])


def _pool_stream(h, x, n_seq, seq_len, row0, mod_row0, mod_rows_per_seq, mod, w, b, scale):
    t = jnp.arange(seq_len)
    bands, inv_counts = [], []
    for win in POOL_WINDOWS:
        lo = jnp.clip(t - win // 2, 0, seq_len)
        hi = jnp.clip(t + win // 2, 0, seq_len)
        bands.append(((t[None, :] >= lo[:, None]) & (t[None, :] < hi[:, None])).astype(BF16))
        inv_counts.append((1.0 / (hi - lo).astype(F32))[:, None])
    band = jnp.stack(bands)
    inv_count = jnp.stack(inv_counts)
    gd = POOL_GROUP_DIM
    seq0 = row0 // seq_len
    per = D // gd
    return pl.pallas_call(
        _pool_kernel,
        grid=(n_seq, len(POOL_WINDOWS)),
        in_specs=[
            pl.BlockSpec((seq_len, gd), lambda s, gi: (seq0 + s, gi)),
            pl.BlockSpec((seq_len, gd), lambda s, gi: (seq0 + s, gi)),
            pl.BlockSpec((None, 1, gd), lambda s, gi: (mod_row0 + s * mod_rows_per_seq, 0, 2 * per + gi)),
            pl.BlockSpec((None, seq_len, seq_len), lambda s, gi: (gi, 0, 0)),
            pl.BlockSpec((None, seq_len, 1), lambda s, gi: (gi, 0, 0)),
            pl.BlockSpec((None, gd, gd), lambda s, gi: (gi, 0, 0)),
            pl.BlockSpec((1, gd), lambda s, gi: (0, gi)),
            pl.BlockSpec((1, gd), lambda s, gi: (0, gi)),
        ],
        out_specs=pl.BlockSpec((seq_len, gd), lambda s, gi: (s, gi)),
        out_shape=jax.ShapeDtypeStruct((n_seq * seq_len, D), F32),
        compiler_params=_params(("arbitrary", "arbitrary")),
        name="pool",
    )(h, x, mod, band, inv_count, w, b.reshape(1, D), scale.reshape(1, D))


def _pool_layer(g, x, mod, norm_g, w, b, scale):
    nc = _ctx_rows(g)
    h = _prenorm_call(g, x, norm_g, mod, 0)
    xc = _pool_stream(h, x, g.n_ctx, g.ctx_len, 0, 0, 0, mod, w, b, scale)
    xl = _pool_stream(h, x, g.n_lat, g.lat_len, nc, 1, 1, mod, w, b, scale)
    return jnp.concatenate([xc, xl], axis=0)


def _split_bf16(a):
    hi = a.astype(BF16)
    return hi, (a - hi.astype(F32)).astype(BF16)


def _pack_bf16_pairs(h):
    bits = lax.bitcast_convert_type(h.astype(BF16).astype(F32), jnp.uint32)
    half = h.shape[1] // 2
    return (bits[:, :half] >> 16) | bits[:, half:]


def _unpack_bf16_pairs(w):
    lo = lax.bitcast_convert_type(w << 16, F32)
    hi = lax.bitcast_convert_type(w & jnp.uint32(0xFFFF0000), F32)
    return jnp.concatenate([lo, hi], axis=1).astype(BF16)


ROW_TILE = (D // 2 // LANES, LANES)


def _store_row_tiles(ref, packed):
    for s in range(ROW_TILE[0]):
        ref[:, s, :] = packed[:, s * LANES : (s + 1) * LANES]


def _load_row_tiles(ref, rows=slice(None)):
    return jnp.concatenate([ref[rows, s, :] for s in range(ROW_TILE[0])], axis=1)


def _router_kernel(x_ref, g_ref, sh_ref, sc_ref, wr_ref, br_ref, h_ref, idx_ref, pos_ref, gate_ref, cnt_ref,
                   carry_ref):
    @pl.when(pl.program_id(0) == 0)
    def _():
        carry_ref[...] = jnp.zeros_like(carry_ref)

    h = _prenorm(x_ref[...], g_ref[...], sh_ref[...], sc_ref[...])
    _store_row_tiles(h_ref, _pack_bf16_pairs(h))
    h_hi, h_lo = _split_bf16(h)
    w_hi, w_lo = _split_bf16(wr_ref[...])
    dot = functools.partial(jnp.dot, preferred_element_type=F32)
    logits = dot(h_hi, w_hi) + dot(h_hi, w_lo) + dot(h_lo, w_hi) + br_ref[...]

    lane = lax.broadcasted_iota(I32, logits.shape, 1).astype(F32)
    vals, hots = [], []
    for k in range(TOP_K):
        top = jnp.max(logits, axis=-1, keepdims=True)
        sel = jnp.min(jnp.where(logits == top, lane, float(N_EXPERTS)), axis=-1, keepdims=True)
        hot = lane == sel
        idx_ref[:, k : k + 1] = sel.astype(I32)
        vals.append(top)
        hots.append(hot)
        logits = jnp.where(hot, -jnp.inf, logits)
    exps = [jnp.exp(v - vals[0]) for v in vals]
    denom = exps[0] + exps[1] + exps[2] + exps[3]
    for k in range(TOP_K):
        gate_ref[:, k : k + 1] = exps[k] / denom

    hot_all = sum(jnp.where(hot, 1.0, 0.0) for hot in hots)
    r = lax.broadcasted_iota(I32, (TM, TM), 0)
    c = lax.broadcasted_iota(I32, (TM, TM), 1)
    earlier = jnp.where(r > c, 1.0, 0.0).astype(BF16)
    base = dot(earlier, hot_all.astype(BF16)) + carry_ref[...]
    for k in range(TOP_K):
        pos_ref[:, k : k + 1] = jnp.sum(jnp.where(hots[k], base, 0.0), axis=-1, keepdims=True).astype(I32)
    carry_ref[...] = carry_ref[...] + jnp.sum(hot_all, axis=0, keepdims=True)
    cnt_ref[...] = carry_ref[...]


ROW_DMA_UNROLL = 8
PAIRS_PER_TILE = TM * TOP_K


def _dispatch_kernel(dest_ref, h_ref, xs_ref, sem):
    base = pl.program_id(0) * PAIRS_PER_TILE

    def issue(r, carry):
        for k in range(TOP_K):
            d = dest_ref[base + r * TOP_K + k]
            pltpu.make_async_copy(h_ref.at[r], xs_ref.at[d], sem).start()
        return carry

    lax.fori_loop(0, TM, issue, 0, unroll=ROW_DMA_UNROLL)
    rows = xs_ref.at[pl.ds(0, PAIRS_PER_TILE)]
    pltpu.make_async_copy(rows, rows, sem).wait()


def _expert_mm_kernel(layer, col_tiles, bn, compute, be_ref, br_ref, first_ref, nv_ref, na_ref, run_ref, nxt_ref,
                      nruns_ref, x_ref, w_hbm, *rest):
    n_w = len(col_tiles)
    bias_refs, o_ref, (wbuf, wbf, sems) = rest[:n_w], rest[n_w], rest[n_w + 1 :]
    n, b = pl.program_id(0), pl.program_id(1)
    active = b < na_ref[0]

    def w_copy(slot, nn, e, j):
        col = pl.multiple_of((col_tiles[j] + nn) * bn, bn)
        return pltpu.make_async_copy(w_hbm.at[layer, e, :, pl.ds(col, bn)], wbuf.at[slot, j], sems.at[slot])

    def prefetch(slot, nn, e):
        for j in range(n_w):
            w_copy(slot, nn, e, j).start(priority=WEIGHT_DMA_PRIORITY)

    @pl.when(jnp.logical_and(active, first_ref[b] == 1))
    def _():
        visit = n * nruns_ref[0] + run_ref[b]
        slot = visit % 2

        @pl.when(visit == 0)
        def _():
            prefetch(slot, n, be_ref[b])

        for j in range(n_w):
            w_copy(slot, n, be_ref[b], j).wait()
        nxt = nxt_ref[b]

        @pl.when(nxt >= 0)
        def _():
            prefetch(1 - slot, n, nxt)

        @pl.when(jnp.logical_and(nxt < 0, n + 1 < pl.num_programs(0)))
        def _():
            prefetch(1 - slot, n + 1, be_ref[0])

        for j in range(n_w):
            wbf[j] = wbuf[slot, j].astype(BF16)

    @pl.when(active)
    def _():
        compute(x_ref, wbf, bias_refs, o_ref, nv_ref[b])


def _expert_up_compute(x_ref, wbf, bias_refs, o_ref, n_valid):
    row = lax.broadcasted_iota(I32, (MOE_BM, 1), 0)
    x = _unpack_bf16_pairs(jnp.where(row < n_valid, _load_row_tiles(x_ref), jnp.uint32(0)))
    gate = jnp.dot(x, wbf[0], preferred_element_type=F32) + bias_refs[0][...]
    up = jnp.dot(x, wbf[1], preferred_element_type=F32) + bias_refs[1][...]
    gate = jnp.minimum(gate, SWIGLU_LIMIT)
    up = jnp.clip(up, -SWIGLU_LIMIT, SWIGLU_LIMIT)
    glu = gate * (1.0 / (1.0 + jnp.exp(-SWIGLU_ALPHA * gate)))
    o_ref[...] = ((up + 1.0) * glu).astype(BF16)


def _expert_down_compute(h_ref, wbf, bias_refs, o_ref, n_valid):
    _store_row_tiles(o_ref, _pack_bf16_pairs(jnp.dot(h_ref[...], wbf[0], preferred_element_type=F32) + bias_refs[0][...]))


def _expert_mm(layer, sched, n_blocks, x, w, bias4, col_tiles, n_col_tiles, bn, compute, out_tail, out_dtype, name):
    k = w.shape[2]
    n_w = len(col_tiles)
    n_rows = n_blocks * MOE_BM
    x_tail = x.shape[1:]
    zeros = lambda t: (0,) * (len(t) - 1)
    bias_spec = lambda off: pl.BlockSpec((None, None, 1, bn), lambda n, b, be, *_: (layer, be[b], 0, n + off))
    return pl.pallas_call(
        functools.partial(_expert_mm_kernel, layer, col_tiles, bn, compute),
        grid_spec=pltpu.PrefetchScalarGridSpec(
            num_scalar_prefetch=len(sched),
            grid=(n_col_tiles, n_blocks),
            in_specs=[pl.BlockSpec((MOE_BM,) + x_tail, lambda n, b, be, br, *_: (br[b],) + zeros(x_tail) + (0,)),
                      pl.BlockSpec(memory_space=pl.ANY)] + [bias_spec(off) for off in col_tiles],
            out_specs=pl.BlockSpec((MOE_BM,) + out_tail, lambda n, b, be, br, *_: (br[b],) + zeros(out_tail) + (n,)),
            scratch_shapes=[pltpu.VMEM((2, n_w, k, bn), F32), pltpu.VMEM((n_w, k, bn), BF16),
                            pltpu.SemaphoreType.DMA((2,))],
        ),
        out_shape=jax.ShapeDtypeStruct((n_rows,) + out_tail[:-1] + (n_col_tiles * out_tail[-1],), out_dtype),
        compiler_params=_params(("arbitrary", "arbitrary")),
        name=name,
    )(*sched, x, w, *([bias4] * n_w))


def _combine_kernel(dest_ref, ys_ref, gate_ref, x_ref, gmod_ref, o_ref, buf, sems):
    i = pl.program_id(0)

    def issue_tile(tile, slot):
        base = tile * PAIRS_PER_TILE

        def issue(r, carry):
            for k in range(TOP_K):
                d = dest_ref[base + r * TOP_K + k]
                pltpu.make_async_copy(ys_ref.at[d], buf.at[slot, k * TM + r], sems.at[slot]).start()
            return carry

        lax.fori_loop(0, TM, issue, 0, unroll=ROW_DMA_UNROLL)

    @pl.when(i == 0)
    def _():
        issue_tile(0, 0)

    slot = i % 2

    @pl.when(i + 1 < pl.num_programs(0))
    def _():
        issue_tile(i + 1, 1 - slot)

    pltpu.make_async_copy(ys_ref.at[pl.ds(0, PAIRS_PER_TILE)], buf.at[slot], sems.at[slot]).wait()
    gates = gate_ref[...]
    rows_of = lambda k: _unpack_bf16_pairs(_load_row_tiles(buf.at[slot], slice(k * TM, (k + 1) * TM))).astype(F32)
    y = gates[:, 0:1] * rows_of(0)
    for k in range(1, TOP_K):
        y = y + gates[:, k : k + 1] * rows_of(k)
    o_ref[...] = x_ref[...] + gmod_ref[...] * y


def _moe_layer(g, layer, x, mod, norm_g, w_router, b_router, w_gate_up, b_gate_up, w_down, b_down):
    m = x.shape[0]
    n_tiles = m // TM
    depth = w_gate_up.shape[0]
    const = lambda shape: pl.BlockSpec(shape, lambda i: (0,) * len(shape))
    tile4 = pl.BlockSpec((TM, TOP_K), lambda i: (i, 0))
    h2, idx, pos, gates, counts = pl.pallas_call(
        _router_kernel,
        grid=(n_tiles,),
        in_specs=[pl.BlockSpec((TM, D), lambda i: (i, 0)), const((1, D)), _mod_spec(g, 3), _mod_spec(g, 4),
                  const((D, N_EXPERTS)), const((1, N_EXPERTS))],
        out_specs=(pl.BlockSpec((TM,) + ROW_TILE, lambda i: (i, 0, 0)), tile4, tile4, tile4, const((1, N_EXPERTS))),
        out_shape=(jax.ShapeDtypeStruct((m,) + ROW_TILE, jnp.uint32), jax.ShapeDtypeStruct((m, TOP_K), I32),
                   jax.ShapeDtypeStruct((m, TOP_K), I32), jax.ShapeDtypeStruct((m, TOP_K), F32),
                   jax.ShapeDtypeStruct((1, N_EXPERTS), F32)),
        scratch_shapes=[pltpu.VMEM((1, N_EXPERTS), F32)],
        compiler_params=_params(("arbitrary",)),
        name="moe_router",
    )(x, norm_g.reshape(1, D), mod, mod, w_router, b_router.reshape(1, N_EXPERTS))

    n_pairs = m * TOP_K
    n_blocks = -(-(n_pairs + N_EXPERTS * (MOE_BM - 1)) // MOE_BM)
    n_rows = n_blocks * MOE_BM
    counts = counts.reshape(N_EXPERTS).astype(I32)
    padded = (counts + MOE_BM - 1) // MOE_BM * MOE_BM
    pad_end = jnp.cumsum(padded)
    pad_start = pad_end - padded
    dest = (pad_start[idx] + pos).reshape(n_pairs)
    blk = jnp.arange(n_blocks, dtype=I32)
    n_active = (pad_end[-1] // MOE_BM).astype(I32)
    blk_row = jnp.minimum(blk, n_active - 1)
    blk_exp = jnp.minimum(jnp.sum(pad_end[None, :] <= (blk_row * MOE_BM)[:, None], axis=1), N_EXPERTS - 1).astype(I32)
    first = jnp.concatenate([jnp.ones((1,), I32), (blk_exp[1:] != blk_exp[:-1]).astype(I32)])
    n_valid = jnp.clip(counts[blk_exp] - (blk * MOE_BM - pad_start[blk_exp]), 0, MOE_BM).astype(I32)
    run_idx = jnp.cumsum(first) - 1
    n_runs = jnp.sum(first).astype(I32)
    has_rows = counts > 0
    rank = jnp.cumsum(has_rows) - 1
    slot_of = jnp.arange(N_EXPERTS + 1, dtype=I32)[:, None] == jnp.where(has_rows, rank, -1)[None, :]
    used = jnp.sum(jnp.where(slot_of, jnp.arange(N_EXPERTS, dtype=I32)[None, :] + 1, 0), axis=1).astype(I32) - 1
    nxt_exp = used[run_idx + 1]
    sched = (blk_exp, blk_row, first, n_valid, n_active.reshape(1), run_idx.astype(I32), nxt_exp, n_runs.reshape(1))

    xs = pl.pallas_call(
        _dispatch_kernel,
        grid_spec=pltpu.PrefetchScalarGridSpec(
            num_scalar_prefetch=1,
            grid=(n_tiles,),
            in_specs=[pl.BlockSpec((TM,) + ROW_TILE, lambda i, d: (i, 0, 0))],
            out_specs=pl.BlockSpec(memory_space=pl.ANY),
            scratch_shapes=[pltpu.SemaphoreType.DMA],
        ),
        out_shape=jax.ShapeDtypeStruct((n_rows,) + ROW_TILE, jnp.uint32),
        compiler_params=_params(("arbitrary",)),
        name="moe_dispatch",
    )(dest, h2)

    e_tiles = EXPERT_DIM // MOE_UP_BN
    hid = _expert_mm(layer, sched, n_blocks, xs, w_gate_up, b_gate_up.reshape(depth, N_EXPERTS, 1, 2 * EXPERT_DIM),
                     (0, e_tiles), e_tiles, MOE_UP_BN, _expert_up_compute, (MOE_UP_BN,), BF16, "moe_up")
    ys = _expert_mm(layer, sched, n_blocks, hid, w_down, b_down.reshape(depth, N_EXPERTS, 1, D),
                    (0,), D // MOE_DOWN_BN, MOE_DOWN_BN, _expert_down_compute, ROW_TILE, jnp.uint32, "moe_down")

    return pl.pallas_call(
        _combine_kernel,
        grid_spec=pltpu.PrefetchScalarGridSpec(
            num_scalar_prefetch=1,
            grid=(n_tiles,),
            in_specs=[pl.BlockSpec(memory_space=pl.ANY),
                      pl.BlockSpec((TM, TOP_K), lambda i, d: (i, 0)),
                      pl.BlockSpec((TM, D), lambda i, d: (i, 0)),
                      pl.BlockSpec((None, 1, D), lambda i, d: (_mod_row(g, i), 0, 5))],
            out_specs=pl.BlockSpec((TM, D), lambda i, d: (i, 0)),
            scratch_shapes=[pltpu.VMEM((2, PAIRS_PER_TILE) + ROW_TILE, jnp.uint32), pltpu.SemaphoreType.DMA((2,))],
        ),
        out_shape=jax.ShapeDtypeStruct((m, D), F32),
        compiler_params=_params(("arbitrary",)),
        name="moe_combine",
    )(dest, ys, gates, x, mod)


def _trunk(g, x_prompt, x_sample, c, c_ctx, cache_mla_ckv, cache_mla_krope, cache_diff_k, cache_diff_v,
           norm1_g, norm2_g, w_mod, b_mod, moe_w_router, moe_b_router, moe_w_gate_up, moe_b_gate_up,
           moe_w_down, moe_b_down, gmlp, mla, diff, pool):
    depth = w_mod.shape[0]
    nc, nl = _ctx_rows(g), _lat_rows(g)
    x = jnp.concatenate([x_prompt.reshape(nc, D), x_sample.reshape(nl, D)], axis=0)
    cond8 = jnp.concatenate([c_ctx.reshape(1, D), c, jnp.zeros((8 - 1 - g.n_lat, D), F32)], axis=0)
    mods = _modulation(cond8, w_mod, b_mod)
    states = {}
    for l in range(depth):
        kind, j = l % 4, l // 4
        mod = mods[l]
        if kind == 0:
            x = _gmlp_layer(g, x, mod, norm1_g[l], *[p[j] for p in gmlp])
        elif kind == 1:
            x, ckv, krope = _mla_layer(g, x, mod, norm1_g[l], cache_mla_ckv[:, j], cache_mla_krope[:, j],
                                       *[p[j] for p in mla])
            states.setdefault("ckv", []).append(ckv.reshape(g.n_ctx, g.ctx_len, MLA_RANK))
            states.setdefault("krope", []).append(krope.reshape(g.n_ctx, g.ctx_len, MLA_ROPE))
        elif kind == 2:
            lam_init = 0.8 - 0.6 * math.exp(-0.3 * l)
            x, dk, dv = _diff_layer(g, x, mod, norm1_g[l], cache_diff_k[:, j], cache_diff_v[:, j],
                                    *[p[j] for p in diff], lam_init)
            shape = (g.n_ctx, g.ctx_len, DIFF_HEADS, 2 * DIFF_DIM)
            states.setdefault("dk", []).append(dk.reshape(shape))
            states.setdefault("dv", []).append(dv.reshape(shape))
        else:
            x = _pool_layer(g, x, mod, norm1_g[l], *[p[j] for p in pool])
        x = _moe_layer(g, l, x, mod, norm2_g[l], moe_w_router[l], moe_b_router[l], moe_w_gate_up, moe_b_gate_up,
                       moe_w_down, moe_b_down)
    y_prompt = x[:nc].reshape(x_prompt.shape)
    y_sample = x[nc:].reshape(x_sample.shape)
    return (y_prompt, y_sample, jnp.stack(states["ckv"], axis=1), jnp.stack(states["krope"], axis=1),
            jnp.stack(states["dk"], axis=1), jnp.stack(states["dv"], axis=1))


def kernel(x_prompt, x_sample, c, c_ctx, cache_mla_ckv, cache_mla_krope, cache_diff_k, cache_diff_v, norm1_g, norm2_g, w_mod, b_mod, moe_w_router, moe_b_router, moe_w_gate_up, moe_b_gate_up, moe_w_down, moe_b_down, gmlp_w_in, gmlp_b_in, gmlp_ln_g, gmlp_ln_b, gmlp_w_s, gmlp_b_s, gmlp_w_out, gmlp_b_out, mla_w_dq, mla_g_qa, mla_w_uq, mla_w_dkv, mla_g_kva, mla_w_kr, mla_w_uk, mla_w_uv, mla_g_q, mla_g_k, mla_w_o, diff_w_qkv, diff_g_q, diff_g_k, diff_lambda, diff_g_sub, diff_w_o, pool_w, pool_b, pool_scale):
    g = Geom(x_prompt.shape[0], x_prompt.shape[1], x_sample.shape[0], x_sample.shape[1], cache_mla_ckv.shape[2])
    gmlp = (gmlp_w_in, gmlp_b_in, gmlp_ln_g, gmlp_ln_b, gmlp_w_s, gmlp_b_s, gmlp_w_out, gmlp_b_out)
    mla = (mla_w_dq, mla_g_qa, mla_w_uq, mla_w_dkv, mla_g_kva, mla_w_kr, mla_w_uk, mla_w_uv, mla_g_q, mla_g_k,
           mla_w_o)
    diff = (diff_w_qkv, diff_g_q, diff_g_k, diff_lambda, diff_g_sub, diff_w_o)
    pool = (pool_w, pool_b, pool_scale)
    return _trunk(g, x_prompt, x_sample, c, c_ctx, cache_mla_ckv, cache_mla_krope, cache_diff_k, cache_diff_v,
                  norm1_g, norm2_g, w_mod, b_mod, moe_w_router, moe_b_router, moe_w_gate_up, moe_b_gate_up,
                  moe_w_down, moe_b_down, gmlp, mla, diff, pool)
```

```python
import collections
import functools
import math

import jax
import jax.numpy as jnp
from jax import lax
from jax.experimental import pallas as pl
from jax.experimental.pallas import tpu as pltpu

F32 = jnp.float32
BF16 = jnp.bfloat16
I32 = jnp.int32

D = 2048
EPS = 1e-6
ROPE_THETA = 10000.0
GRID_W = 64
TM = 256
MM_BM = 512
MM_BN = 1024
LANES = 128
VMEM_LIMIT = 56 * 1024 * 1024

GMLP_CHUNK = 128
GMLP_GROUPS = 16
MLA_HEADS = 16
MLA_RANK = 512
MLA_NOPE = 128
MLA_ROPE = 64
MLA_QK = MLA_NOPE + MLA_ROPE
DIFF_HEADS = 8
DIFF_DIM = 128
POOL_WINDOWS = (2, 4, 8, 16)
POOL_GROUP_DIM = 512
N_EXPERTS = 32
TOP_K = 4
EXPERT_DIM = 2048
SWIGLU_LIMIT = 7.0
SWIGLU_ALPHA = 1.702
MOE_BM = 256
MOE_UP_BN = 1024
MOE_DOWN_BN = 2048
WEIGHT_DMA_PRIORITY = 1

Geom = collections.namedtuple("Geom", "n_ctx ctx_len n_lat lat_len past_len")


def _ctx_rows(g):
    return g.n_ctx * g.ctx_len


def _lat_rows(g):
    return g.n_lat * g.lat_len


def _mod_row(g, i, bm=TM):
    ct = _ctx_rows(g) // bm
    return jnp.where(i < ct, 0, 1 + (i - ct) // (g.lat_len // bm))


def _rope_tile(g, i, bm):
    ct = _ctx_rows(g) // bm
    return jnp.where(i < ct, 0, 1 + (i - ct) % (g.lat_len // bm))


def _params(sem):
    return pltpu.CompilerParams(dimension_semantics=sem, vmem_limit_bytes=VMEM_LIMIT)


def _mod_spec(g, k, n_axis=None, bn=D):
    per = D // bn
    if n_axis is None:
        return pl.BlockSpec((None, 1, bn), lambda m: (_mod_row(g, m), 0, k * per))
    return pl.BlockSpec((None, 1, bn), lambda n, m: (_mod_row(g, m, MM_BM), 0, k * per + n))


def _rms(x, gain):
    return x * lax.rsqrt(jnp.mean(x * x, axis=-1, keepdims=True) + EPS) * gain


def _prenorm(x, gain, shift, scale):
    return _rms(x, gain) * (1.0 + scale) + shift


def _modulation_kernel(c_ref, w_ref, b_ref, o_ref):
    c = c_ref[...]
    s = c * (1.0 / (1.0 + jnp.exp(-c)))
    o_ref[...] = jnp.dot(s.astype(BF16), w_ref[...].astype(BF16), preferred_element_type=F32) + b_ref[...]


def _modulation(cond8, w_mod, b_mod):
    depth = w_mod.shape[0]
    bn = 1024
    out = pl.pallas_call(
        _modulation_kernel,
        grid=(depth, 6 * D // bn),
        in_specs=[
            pl.BlockSpec((8, D), lambda l, n: (0, 0)),
            pl.BlockSpec((None, D, bn), lambda l, n: (l, 0, n)),
            pl.BlockSpec((None, 1, bn), lambda l, n: (l, 0, n)),
        ],
        out_specs=pl.BlockSpec((None, 8, bn), lambda l, n: (l, 0, n)),
        out_shape=jax.ShapeDtypeStruct((depth, 8, 6 * D), F32),
        compiler_params=_params(("arbitrary", "arbitrary")),
        name="modulation",
    )(cond8, w_mod, b_mod.reshape(depth, 1, 6 * D))
    return out.reshape(depth, 8, 1, 6 * D)


def _prenorm_kernel(x_ref, g_ref, sh_ref, sc_ref, o_ref):
    o_ref[...] = _prenorm(x_ref[...], g_ref[...], sh_ref[...], sc_ref[...]).astype(o_ref.dtype)


def _prenorm_call(g, x, gain, mod, k_shift):
    m = x.shape[0]
    return pl.pallas_call(
        _prenorm_kernel,
        grid=(m // TM,),
        in_specs=[
            pl.BlockSpec((TM, D), lambda i: (i, 0)),
            pl.BlockSpec((1, D), lambda i: (0, 0)),
            _mod_spec(g, k_shift),
            _mod_spec(g, k_shift + 1),
        ],
        out_specs=pl.BlockSpec((TM, D), lambda i: (i, 0)),
        out_shape=jax.ShapeDtypeStruct((m, D), BF16),
        compiler_params=_params(("arbitrary",)),
        name="prenorm",
    )(x, gain.reshape(1, D), mod, mod)


def _mm_kernel(n_extra, epilogue, x_ref, w_ref, *rest):
    extras, outs, wbf_ref = rest[:n_extra], rest[n_extra:-1], rest[-1]

    @pl.when(pl.program_id(1) == 0)
    def _():
        wbf_ref[...] = w_ref[...].astype(BF16)

    acc = jnp.dot(x_ref[...].astype(BF16), wbf_ref[...], preferred_element_type=F32)
    epilogue(acc, extras, outs)


def _mm(x, w, ncols, bn, epilogue, extras, out_shape, out_specs, name, w_off=0):
    m, k = x.shape
    return pl.pallas_call(
        functools.partial(_mm_kernel, len(extras), epilogue),
        grid=(ncols // bn, m // MM_BM),
        in_specs=[
            pl.BlockSpec((MM_BM, k), lambda n, i: (i, 0)),
            pl.BlockSpec((k, bn), lambda n, i: (0, n + w_off)),
        ]
        + [s for _, s in extras],
        out_specs=out_specs,
        out_shape=out_shape,
        scratch_shapes=[pltpu.VMEM((k, bn), BF16)],
        compiler_params=_params(("arbitrary", "arbitrary")),
        name=name,
    )(x, w, *[a for a, _ in extras])


def _row_spec(bn):
    return pl.BlockSpec((1, bn), lambda n, i: (0, n))


def _tile_spec(bn):
    return pl.BlockSpec((MM_BM, bn), lambda n, i: (i, n))


def _residual_epilogue(acc, extras, outs):
    b_ref, x_ref, gate_ref = extras
    outs[0][...] = x_ref[...] + gate_ref[...] * (acc + b_ref[...])


def _mm_residual(g, h, w, bias, x, mod, k_gate, name):
    m = x.shape[0]
    bn = MM_BN
    extras = [(bias.reshape(1, D), _row_spec(bn)), (x, _tile_spec(bn)), (mod, _mod_spec(g, k_gate, 0, bn))]
    return _mm(h, w, D, bn, _residual_epilogue, extras, jax.ShapeDtypeStruct((m, D), F32), _tile_spec(bn), name)


def _gelu_epilogue(acc, extras, outs):
    z = acc + extras[0][...]
    outs[0][...] = (0.5 * z * (1.0 + lax.erf(z * (2.0 ** -0.5)))).astype(BF16)


def _gmlp_gate_kernel(u_ref, v_ref, lg_ref, lb_ref, ws_ref, bs_ref, o_ref):
    v = v_ref[...].astype(F32)
    mu = jnp.mean(v, axis=-1, keepdims=True)
    vc = v - mu
    var = jnp.mean(vc * vc, axis=-1, keepdims=True)
    vn = (vc * lax.rsqrt(var + EPS) * lg_ref[...] + lb_ref[...]).astype(BF16)
    for grp in range(GMLP_GROUPS):
        cols = slice(grp * LANES, (grp + 1) * LANES)
        w = ws_ref[grp].astype(BF16)
        bias = bs_ref[:, grp : grp + 1]
        for c in range(TM // GMLP_CHUNK):
            rows = slice(c * GMLP_CHUNK, (c + 1) * GMLP_CHUNK)
            vm = jnp.dot(w, vn[rows, cols], preferred_element_type=F32) + bias
            o_ref[rows, cols] = (u_ref[rows, cols].astype(F32) * vm).astype(BF16)


def _gmlp_layer(g, x, mod, norm_g, w_in, b_in, ln_g, ln_b, w_s, b_s, w_out, b_out):
    m = x.shape[0]
    width = D
    h = _prenorm_call(g, x, norm_g, mod, 0)
    bn = MM_BN
    z = _mm(h, w_in, 2 * width, bn, _gelu_epilogue, [(b_in.reshape(1, 2 * width), _row_spec(bn))],
            jax.ShapeDtypeStruct((m, 2 * width), BF16), _tile_spec(bn), "gmlp_in")
    gated = pl.pallas_call(
        _gmlp_gate_kernel,
        grid=(m // TM,),
        in_specs=[
            pl.BlockSpec((TM, width), lambda i: (i, 0)),
            pl.BlockSpec((TM, width), lambda i: (i, 1)),
            pl.BlockSpec((1, width), lambda i: (0, 0)),
            pl.BlockSpec((1, width), lambda i: (0, 0)),
            pl.BlockSpec((GMLP_GROUPS, GMLP_CHUNK, GMLP_CHUNK), lambda i: (0, 0, 0)),
            pl.BlockSpec((GMLP_CHUNK, GMLP_GROUPS), lambda i: (0, 0)),
        ],
        out_specs=pl.BlockSpec((TM, width), lambda i: (i, 0)),
        out_shape=jax.ShapeDtypeStruct((m, width), BF16),
        compiler_params=_params(("arbitrary",)),
        name="gmlp_gate",
    )(z, z, ln_g.reshape(1, width), ln_b.reshape(1, width), w_s, b_s.T)
    return _mm_residual(g, gated, w_out, b_out, x, mod, 2, "gmlp_out")


def _rope_tables(g, d):
    nf = d // 4
    t = jnp.arange(g.lat_len)
    row = (t // GRID_W).astype(F32)
    col = (t % GRID_W).astype(F32)
    inv = ROPE_THETA ** (-jnp.arange(nf, dtype=F32) / nf)
    ang_r = row[:, None] * inv[None, :]
    ang_c = col[:, None] * inv[None, :]
    cos = jnp.concatenate([jnp.cos(ang_r)] * 2 + [jnp.cos(ang_c)] * 2, axis=-1)
    sin = jnp.concatenate([-jnp.sin(ang_r), jnp.sin(ang_r), -jnp.sin(ang_c), jnp.sin(ang_c)], axis=-1)
    reps = LANES // d
    cos = jnp.tile(cos, (1, reps))
    sin = jnp.tile(sin, (1, reps))
    cos = jnp.concatenate([jnp.ones((MM_BM, LANES), F32), cos], axis=0)
    sin = jnp.concatenate([jnp.zeros((MM_BM, LANES), F32), sin], axis=0)
    return cos, sin


def _rope_specs(g):
    return pl.BlockSpec((MM_BM, LANES), lambda n, i: (_rope_tile(g, i, MM_BM), 0))


def _rope128(x, cos, sin, nf):
    lane = lax.broadcasted_iota(I32, x.shape, 1)
    swapped = jnp.where((lane % (2 * nf)) < nf, pltpu.roll(x, LANES - nf, 1), pltpu.roll(x, nf, 1))
    return x * cos + swapped * sin


def _softmax_rows(s):
    s = s - jnp.max(s, axis=-1, keepdims=True)
    p = jnp.exp(s)
    return p / jnp.sum(p, axis=-1, keepdims=True)


def _dot_t(a, b):
    return lax.dot_general(a, b, (((1,), (1,)), ((), ())), preferred_element_type=F32)


def _rmsnorm_epilogue(acc, extras, outs):
    outs[0][...] = _rms(acc, extras[0][...]).astype(outs[0].dtype)


def _plain_epilogue(acc, extras, outs):
    outs[0][...] = acc.astype(outs[0].dtype)


def _pair_select(lane_lo, a, b):
    return jnp.where(lane_lo, a, b)


def _mla_q_epilogue(acc, extras, outs):
    gn_ref, gr_ref, cos_ref, sin_ref = extras
    qn_ref, qr_ref = outs
    nope_w = MLA_HEADS * MLA_NOPE
    lane_lo = lax.broadcasted_iota(I32, (acc.shape[0], LANES), 1) < MLA_ROPE
    cos, sin = cos_ref[...], sin_ref[...]
    for pair in range(MLA_HEADS // 2):
        r = acc[:, nope_w + pair * LANES : nope_w + (pair + 1) * LANES]
        r2 = r * r
        ss_lo = jnp.sum(jnp.where(lane_lo, r2, 0.0), axis=-1, keepdims=True)
        ss_hi = jnp.sum(jnp.where(lane_lo, 0.0, r2), axis=-1, keepdims=True)
        rinv = []
        for j, ss_r in enumerate((ss_lo, ss_hi)):
            h = 2 * pair + j
            qn = acc[:, h * LANES : (h + 1) * LANES]
            ri = lax.rsqrt((jnp.sum(qn * qn, axis=-1, keepdims=True) + ss_r) * (1.0 / MLA_QK) + EPS)
            qn_ref[:, h * LANES : (h + 1) * LANES] = (qn * ri * gn_ref[...]).astype(BF16)
            rinv.append(ri)
        rn = r * _pair_select(lane_lo, rinv[0], rinv[1]) * gr_ref[...]
        qr_ref[:, pair * LANES : (pair + 1) * LANES] = _rope128(rn, cos, sin, MLA_ROPE // 4).astype(BF16)


def _mla_kv_epilogue(acc, extras, outs):
    kr_in_ref, gn_ref, gr_ref, cos_ref, sin_ref = extras
    kn_ref, kr_ref, v_ref = outs
    nope_w = MLA_HEADS * MLA_NOPE
    lane_lo = lax.broadcasted_iota(I32, (acc.shape[0], LANES), 1) < MLA_ROPE
    kr = kr_in_ref[...]
    ss_r = jnp.sum(kr * kr, axis=-1, keepdims=True)
    kr2 = jnp.concatenate([kr, kr], axis=-1) * gr_ref[...]
    kr2 = _rope128(kr2, cos_ref[...], sin_ref[...], MLA_ROPE // 4)
    rinv = []
    for h in range(MLA_HEADS):
        kn = acc[:, h * LANES : (h + 1) * LANES]
        ri = lax.rsqrt((jnp.sum(kn * kn, axis=-1, keepdims=True) + ss_r) * (1.0 / MLA_QK) + EPS)
        kn_ref[:, h * LANES : (h + 1) * LANES] = (kn * ri * gn_ref[...]).astype(BF16)
        rinv.append(ri)
    for pair in range(MLA_HEADS // 2):
        scale = _pair_select(lane_lo, rinv[2 * pair], rinv[2 * pair + 1])
        kr_ref[:, pair * LANES : (pair + 1) * LANES] = (kr2 * scale).astype(BF16)
    v_ref[...] = acc[:, nope_w:].astype(BF16)


def _mla_attn_kernel(qn_ref, qr_ref, kn_ref, kr_ref, v_ref, o_ref):
    scale = MLA_QK ** -0.5
    for h in range(MLA_HEADS):
        cn = slice(h * MLA_NOPE, (h + 1) * MLA_NOPE)
        cr = slice(h * MLA_ROPE, (h + 1) * MLA_ROPE)
        s = _dot_t(qn_ref[:, cn], kn_ref[:, cn]) + _dot_t(qr_ref[:, cr], kr_ref[:, cr])
        p = _softmax_rows(s * scale)
        o_ref[:, cn] = jnp.dot(p.astype(BF16), v_ref[:, cn], preferred_element_type=F32).astype(BF16)


def _attention_call(kernel, q_arrays, kv_arrays, n_seq, q_len, kv_len, q_row0, out_width, extras, name):
    qb = q_len // TM
    q_specs = [pl.BlockSpec((TM, a.shape[1]), lambda s, j: (q_row0 // TM + s * qb + j, 0)) for a in q_arrays]
    kv_specs = [pl.BlockSpec((kv_len, a.shape[1]), lambda s, j: (s, 0)) for a in kv_arrays]
    return pl.pallas_call(
        kernel,
        grid=(n_seq, qb),
        in_specs=q_specs + kv_specs + [s for _, s in extras],
        out_specs=pl.BlockSpec((TM, out_width), lambda s, j: (s * qb + j, 0)),
        out_shape=jax.ShapeDtypeStruct((n_seq * q_len, out_width), BF16),
        compiler_params=_params(("arbitrary", "arbitrary")),
        name=name,
    )(*q_arrays, *kv_arrays, *[a for a, _ in extras])


def _latent_kv(g, own, cache):
    nc = _ctx_rows(g)
    own = own[nc:].reshape(g.n_lat, g.lat_len, own.shape[1])
    cache = cache.reshape(g.n_lat, g.past_len, cache.shape[1])
    return jnp.concatenate([cache, own], axis=1).reshape(g.n_lat * (g.past_len + g.lat_len), own.shape[2])


def _mla_layer(g, x, mod, norm_g, cache_ckv, cache_krope, w_dq, g_qa, w_uq, w_dkv, g_kva, w_kr, w_uk, w_uv,
               g_q, g_k, w_o):
    m = x.shape[0]
    nc, nl = _ctx_rows(g), _lat_rows(g)
    h = _prenorm_call(g, x, norm_g, mod, 0)
    cos, sin = _rope_tables(g, MLA_ROPE)
    rope_extras = [(cos, _rope_specs(g)), (sin, _rope_specs(g))]
    full = lambda w: pl.BlockSpec((1, w), lambda n, i: (0, 0))

    qa = _mm(h, w_dq, MLA_RANK, MLA_RANK, _rmsnorm_epilogue, [(g_qa.reshape(1, MLA_RANK), full(MLA_RANK))],
             jax.ShapeDtypeStruct((m, MLA_RANK), BF16), _tile_spec(MLA_RANK), "mla_dq")
    ckv = _mm(h, w_dkv, MLA_RANK, MLA_RANK, _rmsnorm_epilogue, [(g_kva.reshape(1, MLA_RANK), full(MLA_RANK))],
              jax.ShapeDtypeStruct((m, MLA_RANK), F32), _tile_spec(MLA_RANK), "mla_dkv")
    krope = _mm(h, w_kr, MLA_ROPE, MLA_ROPE, _plain_epilogue, [],
                jax.ShapeDtypeStruct((m, MLA_ROPE), F32), _tile_spec(MLA_ROPE), "mla_kr")

    w_uq3 = w_uq.reshape(MLA_RANK, MLA_HEADS, MLA_QK)
    w_uq_p = jnp.concatenate([w_uq3[:, :, :MLA_NOPE].reshape(MLA_RANK, -1),
                              w_uq3[:, :, MLA_NOPE:].reshape(MLA_RANK, -1)], axis=1)
    qw = w_uq_p.shape[1]
    gq_n = g_q[:MLA_NOPE].reshape(1, MLA_NOPE)
    gq_r = jnp.tile(g_q[MLA_NOPE:], 2).reshape(1, LANES)
    nope_w, rope_w = MLA_HEADS * MLA_NOPE, MLA_HEADS * MLA_ROPE
    qn, qr = _mm(qa, w_uq_p, qw, qw, _mla_q_epilogue,
                 [(gq_n, full(LANES)), (gq_r, full(LANES))] + rope_extras,
                 (jax.ShapeDtypeStruct((m, nope_w), BF16), jax.ShapeDtypeStruct((m, rope_w), BF16)),
                 (pl.BlockSpec((MM_BM, nope_w), lambda n, i: (i, 0)), pl.BlockSpec((MM_BM, rope_w), lambda n, i: (i, 0))),
                 "mla_uq")

    n_cache = g.n_lat * g.past_len
    ckv_all = jnp.concatenate([ckv, cache_ckv.reshape(n_cache, MLA_RANK)], axis=0)
    kr_all = jnp.concatenate([krope, cache_krope.reshape(n_cache, MLA_ROPE)], axis=0)
    w_ukv = jnp.concatenate([w_uk, w_uv], axis=1)
    gk_n = g_k[:MLA_NOPE].reshape(1, MLA_NOPE)
    gk_r = jnp.tile(g_k[MLA_NOPE:], 2).reshape(1, LANES)
    m_all = m + n_cache
    n_tok_tiles = m // MM_BM
    kv_rope = pl.BlockSpec((MM_BM, LANES), lambda n, i: (jnp.where(i < n_tok_tiles, _rope_tile(g, i, MM_BM), 0), 0))
    kn, kr, v = _mm(ckv_all, w_ukv, 2 * nope_w, 2 * nope_w, _mla_kv_epilogue,
                    [(kr_all, pl.BlockSpec((MM_BM, MLA_ROPE), lambda n, i: (i, 0))), (gk_n, full(LANES)),
                     (gk_r, full(LANES)), (cos, kv_rope), (sin, kv_rope)],
                    (jax.ShapeDtypeStruct((m_all, nope_w), BF16), jax.ShapeDtypeStruct((m_all, rope_w), BF16),
                     jax.ShapeDtypeStruct((m_all, nope_w), BF16)),
                    (pl.BlockSpec((MM_BM, nope_w), lambda n, i: (i, 0)), pl.BlockSpec((MM_BM, rope_w), lambda n, i: (i, 0)),
                     pl.BlockSpec((MM_BM, nope_w), lambda n, i: (i, 0))),
                    "mla_ukv")

    o_ctx = _attention_call(_mla_attn_kernel, [qn, qr], [kn, kr, v], g.n_ctx, g.ctx_len, g.ctx_len, 0,
                            nope_w, [], "mla_attn_ctx")
    lat_kv = [_latent_kv(g, a[:m], a[m:]) for a in (kn, kr, v)]
    o_lat = _attention_call(_mla_attn_kernel, [qn, qr], lat_kv, g.n_lat, g.lat_len, g.past_len + g.lat_len, nc,
                            nope_w, [], "mla_attn_lat")
    o = jnp.concatenate([o_ctx, o_lat], axis=0)
    x = _mm_residual(g, o, w_o, jnp.zeros((D,), F32), x, mod, 2, "mla_out")
    return x, ckv[:nc], krope[:nc]


def _diff_qk_epilogue(acc, extras, outs):
    g_ref, cos_ref, sin_ref = extras
    cos, sin = cos_ref[...], sin_ref[...]
    for j in range(acc.shape[1] // LANES):
        cols = slice(j * LANES, (j + 1) * LANES)
        y = _rms(acc[:, cols], g_ref[...])
        if len(outs) == 2:
            outs[1][:, cols] = y
        outs[0][:, cols] = _rope128(y, cos, sin, DIFF_DIM // 4).astype(BF16)


def _diff_v_epilogue(acc, extras, outs):
    outs[0][...] = acc.astype(BF16)
    outs[1][...] = acc


def _diff_attn_kernel(lam_init, q_ref, k_ref, v_ref, lam_ref, gs_ref, o_ref):
    lam = lam_ref[...]
    lam_full = (jnp.exp(jnp.sum(lam[0:1] * lam[1:2], axis=-1, keepdims=True))
                - jnp.exp(jnp.sum(lam[2:3] * lam[3:4], axis=-1, keepdims=True)) + lam_init)
    scale = DIFF_DIM ** -0.5
    for h in range(DIFF_HEADS):
        c0 = slice(2 * h * DIFF_DIM, (2 * h + 1) * DIFF_DIM)
        c1 = slice((2 * h + 1) * DIFF_DIM, (2 * h + 2) * DIFF_DIM)
        cv = slice(2 * h * DIFF_DIM, (2 * h + 2) * DIFF_DIM)
        p0 = _softmax_rows(_dot_t(q_ref[:, c0], k_ref[:, c0]) * scale)
        p1 = _softmax_rows(_dot_t(q_ref[:, c1], k_ref[:, c1]) * scale)
        p = p0 - lam_full * p1
        o = jnp.dot(p.astype(BF16), v_ref[:, cv], preferred_element_type=F32)
        o_ref[:, cv] = (_rms(o, gs_ref[...]) * (1.0 - lam_init)).astype(BF16)


def _diff_layer(g, x, mod, norm_g, cache_k, cache_v, w_qkv, g_q, g_k, lam, g_sub, w_o, lam_init):
    m = x.shape[0]
    nc = _ctx_rows(g)
    h = _prenorm_call(g, x, norm_g, mod, 0)
    cos, sin = _rope_tables(g, DIFF_DIM)
    rope_extras = [(cos, _rope_specs(g)), (sin, _rope_specs(g))]
    bn = MM_BN
    gain = lambda a: (a.reshape(1, DIFF_DIM), pl.BlockSpec((1, DIFF_DIM), lambda n, i: (0, 0)))
    (q,) = _mm(h, w_qkv, D, bn, _diff_qk_epilogue, [gain(g_q)] + rope_extras,
               (jax.ShapeDtypeStruct((m, D), BF16),), (_tile_spec(bn),), "diff_q")
    k, k_state = _mm(h, w_qkv, D, bn, _diff_qk_epilogue, [gain(g_k)] + rope_extras,
                     (jax.ShapeDtypeStruct((m, D), BF16), jax.ShapeDtypeStruct((m, D), F32)),
                     (_tile_spec(bn), _tile_spec(bn)), "diff_k", w_off=D // bn)
    v, v_state = _mm(h, w_qkv, D, bn, _diff_v_epilogue, [],
                     (jax.ShapeDtypeStruct((m, D), BF16), jax.ShapeDtypeStruct((m, D), F32)),
                     (_tile_spec(bn), _tile_spec(bn)), "diff_v", w_off=2 * D // bn)
    extras = [(lam, pl.BlockSpec((4, DIFF_DIM), lambda s, j: (0, 0))),
              (g_sub.reshape(1, 2 * DIFF_DIM), pl.BlockSpec((1, 2 * DIFF_DIM), lambda s, j: (0, 0)))]
    kern = functools.partial(_diff_attn_kernel, lam_init)
    o_ctx = _attention_call(kern, [q], [k, v], g.n_ctx, g.ctx_len, g.ctx_len, 0, D, extras, "diff_attn_ctx")
    n_cache = g.n_lat * g.past_len
    k_lat = _latent_kv(g, k, cache_k.reshape(n_cache, D).astype(BF16))
    v_lat = _latent_kv(g, v, cache_v.reshape(n_cache, D).astype(BF16))
    o_lat = _attention_call(kern, [q], [k_lat, v_lat], g.n_lat, g.lat_len, g.past_len + g.lat_len, nc, D,
                            extras, "diff_attn_lat")
    o = jnp.concatenate([o_ctx, o_lat], axis=0)
    x = _mm_residual(g, o, w_o, jnp.zeros((D,), F32), x, mod, 2, "diff_out")
    return x, k_state[:nc], v_state[:nc]


def _pool_kernel(h_ref, x_ref, gate_ref, a_ref, ic_ref, w_ref, b_ref, ps_ref, o_ref):
    hb = h_ref[...]
    win_sum = jnp.dot(a_ref[...], hb, preferred_element_type=F32)
    d = win_sum * ic_ref[...] - hb.astype(F32)
    y = jnp.dot(d.astype(BF16), w_ref[...].astype(BF16), preferred_element_type=F32) + b_ref[...]
    o_ref[...] = x_ref[...] + gate_ref[...] * (y * ps_ref[...])


def _pool_stream(h, x, n_seq, seq_len, row0, mod_row0, mod_rows_per_seq, mod, w, b, scale):
    t = jnp.arange(seq_len)
    bands, inv_counts = [], []
    for win in POOL_WINDOWS:
        lo = jnp.clip(t - win // 2, 0, seq_len)
        hi = jnp.clip(t + win // 2, 0, seq_len)
        bands.append(((t[None, :] >= lo[:, None]) & (t[None, :] < hi[:, None])).astype(BF16))
        inv_counts.append((1.0 / (hi - lo).astype(F32))[:, None])
    band = jnp.stack(bands)
    inv_count = jnp.stack(inv_counts)
    gd = POOL_GROUP_DIM
    seq0 = row0 // seq_len
    per = D // gd
    return pl.pallas_call(
        _pool_kernel,
        grid=(n_seq, len(POOL_WINDOWS)),
        in_specs=[
            pl.BlockSpec((seq_len, gd), lambda s, gi: (seq0 + s, gi)),
            pl.BlockSpec((seq_len, gd), lambda s, gi: (seq0 + s, gi)),
            pl.BlockSpec((None, 1, gd), lambda s, gi: (mod_row0 + s * mod_rows_per_seq, 0, 2 * per + gi)),
            pl.BlockSpec((None, seq_len, seq_len), lambda s, gi: (gi, 0, 0)),
            pl.BlockSpec((None, seq_len, 1), lambda s, gi: (gi, 0, 0)),
            pl.BlockSpec((None, gd, gd), lambda s, gi: (gi, 0, 0)),
            pl.BlockSpec((1, gd), lambda s, gi: (0, gi)),
            pl.BlockSpec((1, gd), lambda s, gi: (0, gi)),
        ],
        out_specs=pl.BlockSpec((seq_len, gd), lambda s, gi: (s, gi)),
        out_shape=jax.ShapeDtypeStruct((n_seq * seq_len, D), F32),
        compiler_params=_params(("arbitrary", "arbitrary")),
        name="pool",
    )(h, x, mod, band, inv_count, w, b.reshape(1, D), scale.reshape(1, D))


def _pool_layer(g, x, mod, norm_g, w, b, scale):
    nc = _ctx_rows(g)
    h = _prenorm_call(g, x, norm_g, mod, 0)
    xc = _pool_stream(h, x, g.n_ctx, g.ctx_len, 0, 0, 0, mod, w, b, scale)
    xl = _pool_stream(h, x, g.n_lat, g.lat_len, nc, 1, 1, mod, w, b, scale)
    return jnp.concatenate([xc, xl], axis=0)


def _split_bf16(a):
    hi = a.astype(BF16)
    return hi, (a - hi.astype(F32)).astype(BF16)


def _pack_bf16_pairs(h):
    bits = lax.bitcast_convert_type(h.astype(BF16).astype(F32), jnp.uint32)
    half = h.shape[1] // 2
    return (bits[:, :half] >> 16) | bits[:, half:]


def _unpack_bf16_pairs(w):
    lo = lax.bitcast_convert_type(w << 16, F32)
    hi = lax.bitcast_convert_type(w & jnp.uint32(0xFFFF0000), F32)
    return jnp.concatenate([lo, hi], axis=1).astype(BF16)


def _router_kernel(x_ref, g_ref, sh_ref, sc_ref, wr_ref, br_ref, h_ref, idx_ref, pos_ref, gate_ref, cnt_ref,
                   carry_ref):
    @pl.when(pl.program_id(0) == 0)
    def _():
        carry_ref[...] = jnp.zeros_like(carry_ref)

    h = _prenorm(x_ref[...], g_ref[...], sh_ref[...], sc_ref[...])
    h_ref[...] = _pack_bf16_pairs(h)
    h_hi, h_lo = _split_bf16(h)
    w_hi, w_lo = _split_bf16(wr_ref[...])
    dot = functools.partial(jnp.dot, preferred_element_type=F32)
    logits = dot(h_hi, w_hi) + dot(h_hi, w_lo) + dot(h_lo, w_hi) + br_ref[...]

    lane = lax.broadcasted_iota(I32, logits.shape, 1).astype(F32)
    vals, hots = [], []
    for k in range(TOP_K):
        top = jnp.max(logits, axis=-1, keepdims=True)
        sel = jnp.min(jnp.where(logits == top, lane, float(N_EXPERTS)), axis=-1, keepdims=True)
        hot = lane == sel
        idx_ref[:, k : k + 1] = sel.astype(I32)
        vals.append(top)
        hots.append(hot)
        logits = jnp.where(hot, -jnp.inf, logits)
    exps = [jnp.exp(v - vals[0]) for v in vals]
    denom = exps[0] + exps[1] + exps[2] + exps[3]
    for k in range(TOP_K):
        gate_ref[:, k : k + 1] = exps[k] / denom

    hot_all = sum(jnp.where(hot, 1.0, 0.0) for hot in hots)
    r = lax.broadcasted_iota(I32, (TM, TM), 0)
    c = lax.broadcasted_iota(I32, (TM, TM), 1)
    earlier = jnp.where(r > c, 1.0, 0.0).astype(BF16)
    base = dot(earlier, hot_all.astype(BF16)) + carry_ref[...]
    for k in range(TOP_K):
        pos_ref[:, k : k + 1] = jnp.sum(jnp.where(hots[k], base, 0.0), axis=-1, keepdims=True).astype(I32)
    carry_ref[...] = carry_ref[...] + jnp.sum(hot_all, axis=0, keepdims=True)
    cnt_ref[...] = carry_ref[...]


ROW_DMA_UNROLL = 8
PAIRS_PER_TILE = TM * TOP_K


def _dispatch_kernel(dest_ref, h_ref, xs_ref, sem):
    base = pl.program_id(0) * PAIRS_PER_TILE

    def issue(r, carry):
        for k in range(TOP_K):
            d = dest_ref[base + r * TOP_K + k]
            pltpu.make_async_copy(h_ref.at[pl.ds(r, 1), :], xs_ref.at[pl.ds(d, 1), :], sem).start(priority=k % 2)
        return carry

    lax.fori_loop(0, TM, issue, 0, unroll=ROW_DMA_UNROLL)
    rows = xs_ref.at[pl.ds(0, PAIRS_PER_TILE), :]
    pltpu.make_async_copy(rows, rows, sem).wait()


def _expert_mm_kernel(layer, col_tiles, bn, compute, be_ref, br_ref, first_ref, nv_ref, na_ref, run_ref, nxt_ref,
                      nruns_ref, x_ref, w_hbm, *rest):
    n_w = len(col_tiles)
    bias_refs, o_ref, (wbuf, wbf, sems) = rest[:n_w], rest[n_w], rest[n_w + 1 :]
    n, b = pl.program_id(0), pl.program_id(1)
    active = b < na_ref[0]

    def w_copy(slot, nn, e, j):
        col = pl.multiple_of((col_tiles[j] + nn) * bn, bn)
        return pltpu.make_async_copy(w_hbm.at[layer, e, :, pl.ds(col, bn)], wbuf.at[slot, j], sems.at[slot])

    def prefetch(slot, nn, e):
        for j in range(n_w):
            w_copy(slot, nn, e, j).start(priority=WEIGHT_DMA_PRIORITY)

    @pl.when(jnp.logical_and(active, first_ref[b] == 1))
    def _():
        visit = n * nruns_ref[0] + run_ref[b]
        slot = visit % 2

        @pl.when(visit == 0)
        def _():
            prefetch(slot, n, be_ref[b])

        for j in range(n_w):
            w_copy(slot, n, be_ref[b], j).wait()
        nxt = nxt_ref[b]

        @pl.when(nxt >= 0)
        def _():
            prefetch(1 - slot, n, nxt)

        @pl.when(jnp.logical_and(nxt < 0, n + 1 < pl.num_programs(0)))
        def _():
            prefetch(1 - slot, n + 1, be_ref[0])

        for j in range(n_w):
            wbf[j] = wbuf[slot, j].astype(BF16)

    @pl.when(active)
    def _():
        compute(x_ref, wbf, bias_refs, o_ref, nv_ref[b])


def _expert_up_compute(x_ref, wbf, bias_refs, o_ref, n_valid):
    row = lax.broadcasted_iota(I32, (MOE_BM, 1), 0)
    x = _unpack_bf16_pairs(jnp.where(row < n_valid, x_ref[...], jnp.uint32(0)))
    gate = jnp.dot(x, wbf[0], preferred_element_type=F32) + bias_refs[0][...]
    up = jnp.dot(x, wbf[1], preferred_element_type=F32) + bias_refs[1][...]
    gate = jnp.minimum(gate, SWIGLU_LIMIT)
    up = jnp.clip(up, -SWIGLU_LIMIT, SWIGLU_LIMIT)
    glu = gate * (1.0 / (1.0 + jnp.exp(-SWIGLU_ALPHA * gate)))
    o_ref[...] = ((up + 1.0) * glu).astype(BF16)


def _expert_down_compute(h_ref, wbf, bias_refs, o_ref, n_valid):
    o_ref[...] = _pack_bf16_pairs(jnp.dot(h_ref[...], wbf[0], preferred_element_type=F32) + bias_refs[0][...])


def _expert_mm(layer, sched, n_blocks, x, w, bias4, col_tiles, n_col_tiles, bn, compute, out_bn, out_dtype, name):
    k = w.shape[2]
    n_w = len(col_tiles)
    n_rows = n_blocks * MOE_BM
    bias_spec = lambda off: pl.BlockSpec((None, None, 1, bn), lambda n, b, be, *_: (layer, be[b], 0, n + off))
    return pl.pallas_call(
        functools.partial(_expert_mm_kernel, layer, col_tiles, bn, compute),
        grid_spec=pltpu.PrefetchScalarGridSpec(
            num_scalar_prefetch=len(sched),
            grid=(n_col_tiles, n_blocks),
            in_specs=[pl.BlockSpec((MOE_BM, x.shape[1]), lambda n, b, be, br, *_: (br[b], 0)),
                      pl.BlockSpec(memory_space=pl.ANY)] + [bias_spec(off) for off in col_tiles],
            out_specs=pl.BlockSpec((MOE_BM, out_bn), lambda n, b, be, br, *_: (br[b], n)),
            scratch_shapes=[pltpu.VMEM((2, n_w, k, bn), F32), pltpu.VMEM((n_w, k, bn), BF16),
                            pltpu.SemaphoreType.DMA((2,))],
        ),
        out_shape=jax.ShapeDtypeStruct((n_rows, n_col_tiles * out_bn), out_dtype),
        compiler_params=_params(("arbitrary", "arbitrary")),
        name=name,
    )(*sched, x, w, *([bias4] * n_w))


def _combine_kernel(dest_ref, ys_ref, gate_ref, x_ref, gmod_ref, o_ref, buf, sems):
    i = pl.program_id(0)

    def issue_tile(tile, slot):
        base = tile * PAIRS_PER_TILE

        def issue(r, carry):
            for k in range(TOP_K):
                d = dest_ref[base + r * TOP_K + k]
                pltpu.make_async_copy(ys_ref.at[pl.ds(d, 1), :], buf.at[slot, pl.ds(k * TM + r, 1), :],
                                      sems.at[slot]).start(priority=k % 2)
            return carry

        lax.fori_loop(0, TM, issue, 0, unroll=ROW_DMA_UNROLL)

    @pl.when(i == 0)
    def _():
        issue_tile(0, 0)

    slot = i % 2

    @pl.when(i + 1 < pl.num_programs(0))
    def _():
        issue_tile(i + 1, 1 - slot)

    pltpu.make_async_copy(ys_ref.at[pl.ds(0, PAIRS_PER_TILE), :], buf.at[slot], sems.at[slot]).wait()
    gates = gate_ref[...]
    y = gates[:, 0:1] * _unpack_bf16_pairs(buf[slot, 0:TM, :]).astype(F32)
    for k in range(1, TOP_K):
        y = y + gates[:, k : k + 1] * _unpack_bf16_pairs(buf[slot, k * TM : (k + 1) * TM, :]).astype(F32)
    o_ref[...] = x_ref[...] + gmod_ref[...] * y


def _moe_layer(g, layer, x, mod, norm_g, w_router, b_router, w_gate_up, b_gate_up, w_down, b_down):
    m = x.shape[0]
    n_tiles = m // TM
    depth = w_gate_up.shape[0]
    const = lambda shape: pl.BlockSpec(shape, lambda i: (0,) * len(shape))
    tile4 = pl.BlockSpec((TM, TOP_K), lambda i: (i, 0))
    h2, idx, pos, gates, counts = pl.pallas_call(
        _router_kernel,
        grid=(n_tiles,),
        in_specs=[pl.BlockSpec((TM, D), lambda i: (i, 0)), const((1, D)), _mod_spec(g, 3), _mod_spec(g, 4),
                  const((D, N_EXPERTS)), const((1, N_EXPERTS))],
        out_specs=(pl.BlockSpec((TM, D // 2), lambda i: (i, 0)), tile4, tile4, tile4, const((1, N_EXPERTS))),
        out_shape=(jax.ShapeDtypeStruct((m, D // 2), jnp.uint32), jax.ShapeDtypeStruct((m, TOP_K), I32),
                   jax.ShapeDtypeStruct((m, TOP_K), I32), jax.ShapeDtypeStruct((m, TOP_K), F32),
                   jax.ShapeDtypeStruct((1, N_EXPERTS), F32)),
        scratch_shapes=[pltpu.VMEM((1, N_EXPERTS), F32)],
        compiler_params=_params(("arbitrary",)),
        name="moe_router",
    )(x, norm_g.reshape(1, D), mod, mod, w_router, b_router.reshape(1, N_EXPERTS))

    n_pairs = m * TOP_K
    n_blocks = -(-(n_pairs + N_EXPERTS * (MOE_BM - 1)) // MOE_BM)
    n_rows = n_blocks * MOE_BM
    counts = counts.reshape(N_EXPERTS).astype(I32)
    padded = (counts + MOE_BM - 1) // MOE_BM * MOE_BM
    pad_end = jnp.cumsum(padded)
    pad_start = pad_end - padded
    dest = (pad_start[idx] + pos).reshape(n_pairs)
    blk = jnp.arange(n_blocks, dtype=I32)
    n_active = (pad_end[-1] // MOE_BM).astype(I32)
    blk_row = jnp.minimum(blk, n_active - 1)
    blk_exp = jnp.minimum(jnp.sum(pad_end[None, :] <= (blk_row * MOE_BM)[:, None], axis=1), N_EXPERTS - 1).astype(I32)
    first = jnp.concatenate([jnp.ones((1,), I32), (blk_exp[1:] != blk_exp[:-1]).astype(I32)])
    n_valid = jnp.clip(counts[blk_exp] - (blk * MOE_BM - pad_start[blk_exp]), 0, MOE_BM).astype(I32)
    run_idx = jnp.cumsum(first) - 1
    n_runs = jnp.sum(first).astype(I32)
    has_rows = counts > 0
    rank = jnp.cumsum(has_rows) - 1
    slot_of = jnp.arange(N_EXPERTS + 1, dtype=I32)[:, None] == jnp.where(has_rows, rank, -1)[None, :]
    used = jnp.sum(jnp.where(slot_of, jnp.arange(N_EXPERTS, dtype=I32)[None, :] + 1, 0), axis=1).astype(I32) - 1
    nxt_exp = used[run_idx + 1]
    sched = (blk_exp, blk_row, first, n_valid, n_active.reshape(1), run_idx.astype(I32), nxt_exp, n_runs.reshape(1))

    xs = pl.pallas_call(
        _dispatch_kernel,
        grid_spec=pltpu.PrefetchScalarGridSpec(
            num_scalar_prefetch=1,
            grid=(n_tiles,),
            in_specs=[pl.BlockSpec((TM, D // 2), lambda i, d: (i, 0))],
            out_specs=pl.BlockSpec(memory_space=pl.ANY),
            scratch_shapes=[pltpu.SemaphoreType.DMA],
        ),
        out_shape=jax.ShapeDtypeStruct((n_rows, D // 2), jnp.uint32),
        compiler_params=_params(("arbitrary",)),
        name="moe_dispatch",
    )(dest, h2)

    e_tiles = EXPERT_DIM // MOE_UP_BN
    hid = _expert_mm(layer, sched, n_blocks, xs, w_gate_up, b_gate_up.reshape(depth, N_EXPERTS, 1, 2 * EXPERT_DIM),
                     (0, e_tiles), e_tiles, MOE_UP_BN, _expert_up_compute, MOE_UP_BN, BF16, "moe_up")
    ys = _expert_mm(layer, sched, n_blocks, hid, w_down, b_down.reshape(depth, N_EXPERTS, 1, D),
                    (0,), D // MOE_DOWN_BN, MOE_DOWN_BN, _expert_down_compute, MOE_DOWN_BN // 2, jnp.uint32, "moe_down")

    return pl.pallas_call(
        _combine_kernel,
        grid_spec=pltpu.PrefetchScalarGridSpec(
            num_scalar_prefetch=1,
            grid=(n_tiles,),
            in_specs=[pl.BlockSpec(memory_space=pl.ANY),
                      pl.BlockSpec((TM, TOP_K), lambda i, d: (i, 0)),
                      pl.BlockSpec((TM, D), lambda i, d: (i, 0)),
                      pl.BlockSpec((None, 1, D), lambda i, d: (_mod_row(g, i), 0, 5))],
            out_specs=pl.BlockSpec((TM, D), lambda i, d: (i, 0)),
            scratch_shapes=[pltpu.VMEM((2, PAIRS_PER_TILE, D // 2), jnp.uint32), pltpu.SemaphoreType.DMA((2,))],
        ),
        out_shape=jax.ShapeDtypeStruct((m, D), F32),
        compiler_params=_params(("arbitrary",)),
        name="moe_combine",
    )(dest, ys, gates, x, mod)


def _trunk(g, x_prompt, x_sample, c, c_ctx, cache_mla_ckv, cache_mla_krope, cache_diff_k, cache_diff_v,
           norm1_g, norm2_g, w_mod, b_mod, moe_w_router, moe_b_router, moe_w_gate_up, moe_b_gate_up,
           moe_w_down, moe_b_down, gmlp, mla, diff, pool):
    depth = w_mod.shape[0]
    nc, nl = _ctx_rows(g), _lat_rows(g)
    x = jnp.concatenate([x_prompt.reshape(nc, D), x_sample.reshape(nl, D)], axis=0)
    cond8 = jnp.concatenate([c_ctx.reshape(1, D), c, jnp.zeros((8 - 1 - g.n_lat, D), F32)], axis=0)
    mods = _modulation(cond8, w_mod, b_mod)
    states = {}
    for l in range(depth):
        kind, j = l % 4, l // 4
        mod = mods[l]
        if kind == 0:
            x = _gmlp_layer(g, x, mod, norm1_g[l], *[p[j] for p in gmlp])
        elif kind == 1:
            x, ckv, krope = _mla_layer(g, x, mod, norm1_g[l], cache_mla_ckv[:, j], cache_mla_krope[:, j],
                                       *[p[j] for p in mla])
            states.setdefault("ckv", []).append(ckv.reshape(g.n_ctx, g.ctx_len, MLA_RANK))
            states.setdefault("krope", []).append(krope.reshape(g.n_ctx, g.ctx_len, MLA_ROPE))
        elif kind == 2:
            lam_init = 0.8 - 0.6 * math.exp(-0.3 * l)
            x, dk, dv = _diff_layer(g, x, mod, norm1_g[l], cache_diff_k[:, j], cache_diff_v[:, j],
                                    *[p[j] for p in diff], lam_init)
            shape = (g.n_ctx, g.ctx_len, DIFF_HEADS, 2 * DIFF_DIM)
            states.setdefault("dk", []).append(dk.reshape(shape))
            states.setdefault("dv", []).append(dv.reshape(shape))
        else:
            x = _pool_layer(g, x, mod, norm1_g[l], *[p[j] for p in pool])
        x = _moe_layer(g, l, x, mod, norm2_g[l], moe_w_router[l], moe_b_router[l], moe_w_gate_up, moe_b_gate_up,
                       moe_w_down, moe_b_down)
    y_prompt = x[:nc].reshape(x_prompt.shape)
    y_sample = x[nc:].reshape(x_sample.shape)
    return (y_prompt, y_sample, jnp.stack(states["ckv"], axis=1), jnp.stack(states["krope"], axis=1),
            jnp.stack(states["dk"], axis=1), jnp.stack(states["dv"], axis=1))


def kernel(x_prompt, x_sample, c, c_ctx, cache_mla_ckv, cache_mla_krope, cache_diff_k, cache_diff_v, norm1_g, norm2_g, w_mod, b_mod, moe_w_router, moe_b_router, moe_w_gate_up, moe_b_gate_up, moe_w_down, moe_b_down, gmlp_w_in, gmlp_b_in, gmlp_ln_g, gmlp_ln_b, gmlp_w_s, gmlp_b_s, gmlp_w_out, gmlp_b_out, mla_w_dq, mla_g_qa, mla_w_uq, mla_w_dkv, mla_g_kva, mla_w_kr, mla_w_uk, mla_w_uv, mla_g_q, mla_g_k, mla_w_o, diff_w_qkv, diff_g_q, diff_g_k, diff_lambda, diff_g_sub, diff_w_o, pool_w, pool_b, pool_scale):
    g = Geom(x_prompt.shape[0], x_prompt.shape[1], x_sample.shape[0], x_sample.shape[1], cache_mla_ckv.shape[2])
    gmlp = (gmlp_w_in, gmlp_b_in, gmlp_ln_g, gmlp_ln_b, gmlp_w_s, gmlp_b_s, gmlp_w_out, gmlp_b_out)
    mla = (mla_w_dq, mla_g_qa, mla_w_uq, mla_w_dkv, mla_g_kva, mla_w_kr, mla_w_uk, mla_w_uv, mla_g_q, mla_g_k,
           mla_w_o)
    diff = (diff_w_qkv, diff_g_q, diff_g_k, diff_lambda, diff_g_sub, diff_w_o)
    pool = (pool_w, pool_b, pool_scale)
    return _trunk(g, x_prompt, x_sample, c, c_ctx, cache_mla_ckv, cache_mla_krope, cache_diff_k, cache_diff_v,
                  norm1_g, norm2_g, w_mod, b_mod, moe_w_router, moe_b_router, moe_w_gate_up, moe_b_gate_up,
                  moe_w_down, moe_b_down, gmlp, mla, diff, pool)
```
